```python
import math
import jax, jax.numpy as jnp
from jax import lax
import numpy as np

D_MODEL = 1024
BATCH = 2
SEQ = 8192
DEPTH = 4
DEC_BATCH = 32
DEC_SEQ = 8
PAST_LEN = 8192
PAGE_SIZE = 128

A_HEADS = 4
A_QK = 64
A_V = 2 * A_QK
A_W = A_HEADS * A_V
B_HEADS = 4
B_DK = 64
B_DV = 128
B_KW = B_HEADS * B_DK
B_VW = B_HEADS * B_DV
GATE_RANK = 16
GATE_TAU = 16.0
GLA_CHUNK = 64
D_FF = 2816
Q_BLOCK = 128
EPS = 1e-6

SPLITS = (A_W, A_W, A_W, B_KW, B_KW, B_VW, GATE_RANK, B_VW, D_MODEL, D_MODEL)
IN_W = sum(SPLITS)

kernel_name = "hybrid_diffattn_gla_macaron_step"


def rmsnorm(x, g):
    xf = x.astype(jnp.float32)
    y = xf * lax.rsqrt(jnp.mean(xf * xf, axis=-1, keepdims=True) + EPS)
    return (y * g.astype(jnp.float32)).astype(x.dtype)


def swiglu(x, wg, wu, wd):
    return (jax.nn.silu(x @ wg) * (x @ wu)) @ wd


def macaron_half(x, g, wg, wu, wd):
    return 0.5 * swiglu(rmsnorm(x, g), wg, wu, wd)


def branch_inputs(h, w_in, w_gate_up, b_gate):
    b_, l_ = h.shape[:2]
    idx = [int(i) for i in np.cumsum(SPLITS)[:-1]]
    qa, ka, va, qb, kb, vb, gdown, r, ma, mb = jnp.split(h @ w_in, idx, axis=-1)
    qa = qa.reshape(b_, l_, A_HEADS, 2, A_QK)
    ka = ka.reshape(b_, l_, A_HEADS, 2, A_QK)
    va = va.reshape(b_, l_, A_HEADS, A_V)
    qb = qb.reshape(b_, l_, B_HEADS, B_DK).astype(jnp.float32) * (B_DK ** -0.5)
    kb = kb.reshape(b_, l_, B_HEADS, B_DK).astype(jnp.float32)
    vb = vb.reshape(b_, l_, B_HEADS, B_DV).astype(jnp.float32)
    log_a = jax.nn.log_sigmoid((gdown @ w_gate_up + b_gate).astype(jnp.float32)) / GATE_TAU
    log_a = log_a.reshape(b_, l_, B_HEADS, B_DK)
    return qa, ka, va, qb, kb, vb, log_a, r, ma, mb


def diff_attn_prompt(q, k, v, lam):
    b_, s_ = q.shape[:2]
    qblk = min(Q_BLOCK, s_)
    nblk = s_ // qblk
    scale = A_QK ** -0.5
    qs = jnp.moveaxis(q.reshape(b_, nblk, qblk, A_HEADS, 2, A_QK), 1, 0)
    kf = k.astype(jnp.float32)
    vf = v.astype(jnp.float32)
    kpos = jnp.arange(s_)

    def block(args):
        qb, i = args
        s = jnp.einsum('bqhmd,bkhmd->bmhqk', qb.astype(jnp.float32), kf) * scale
        qpos = i * qblk + jnp.arange(qblk)
        mask = kpos[None, :] <= qpos[:, None]
        p = jax.nn.softmax(jnp.where(mask, s, -jnp.inf), axis=-1)
        w = p[:, 0] - lam * p[:, 1]
        return jnp.einsum('bhqk,bkhd->bqhd', w, vf)

    o = lax.map(block, (qs, jnp.arange(nblk)))
    return jnp.moveaxis(o, 0, 1).reshape(b_, s_, A_HEADS, A_V)


def diff_attn_sample(q, k, v, cache_k, cache_v, page_table, layer, lam):
    l_ = q.shape[1]
    past = page_table.shape[1] * PAGE_SIZE
    scale = A_QK ** -0.5
    kpos = jnp.arange(past + l_)
    qpos = past + jnp.arange(l_)
    mask = kpos[None, :] <= qpos[:, None]

    def one(args):
        qb, kb, vb, pages = args
        kp = cache_k[layer, pages].reshape(past, A_HEADS, 2, A_QK)
        vp = cache_v[layer, pages].reshape(past, A_HEADS, A_V)
        kall = jnp.concatenate([kp.astype(jnp.float32), kb.astype(jnp.float32)], axis=0)
        vall = jnp.concatenate([vp.astype(jnp.float32), vb.astype(jnp.float32)], axis=0)
        s = jnp.einsum('qhmd,khmd->mhqk', qb.astype(jnp.float32), kall) * scale
        p = jax.nn.softmax(jnp.where(mask, s, -jnp.inf), axis=-1)
        w = p[0] - lam * p[1]
        return jnp.einsum('hqk,khd->qhd', w, vall)

    return lax.map(one, (q, k, v, page_table))


def gla(q, k, v, log_a, s0):
    b_, l_ = q.shape[:2]
    c = min(GLA_CHUNK, l_)
    pad = (-l_) % c
    if pad:
        padw = ((0, 0), (0, pad), (0, 0), (0, 0))
        q, k, v, log_a = [jnp.pad(t, padw) for t in (q, k, v, log_a)]
    n = (l_ + pad) // c

    def chunks(t):
        return jnp.moveaxis(t.reshape(b_, n, c, *t.shape[2:]), 1, 0)

    tri = jnp.arange(c)[:, None] >= jnp.arange(c)[None, :]

    def step(s, inp):
        qc, kc, vc, ac = inp
        bcum = jnp.cumsum(ac, axis=1)
        o_inter = jnp.einsum('bchk,bhkv->bchv', qc * jnp.exp(bcum), s)
        diff = bcum[:, :, None] - bcum[:, None, :]
        dec = jnp.exp(jnp.where(tri[None, :, :, None, None], diff, -jnp.inf))
        att = jnp.einsum('bthk,bshk,btshk->bhts', qc, kc, dec)
        o_intra = jnp.einsum('bhts,bshv->bthv', att, vc)
        blast = bcum[:, -1]
        s_new = jnp.exp(blast)[..., None] * s + jnp.einsum(
            'bshk,bshv->bhkv', kc * jnp.exp(blast[:, None] - bcum), vc)
        return s_new, o_inter + o_intra

    s_fin, o = lax.scan(step, s0.astype(jnp.float32), (chunks(q), chunks(k), chunks(v), chunks(log_a)))
    o = jnp.moveaxis(o, 0, 1).reshape(b_, n * c, B_HEADS, B_DV)[:, :l_]
    return o, s_fin


def merge(o_a, o_b, r, ma, mb, lam_init, a_subln, b_norm, w_pa, w_pb, w_o, dtype):
    b_, l_ = o_a.shape[:2]
    oa = (rmsnorm(o_a, a_subln) * (1.0 - lam_init)).reshape(b_, l_, A_W).astype(dtype)
    ob = (rmsnorm(o_b, b_norm).reshape(b_, l_, B_VW) * jax.nn.silu(r.astype(jnp.float32))).astype(dtype)
    y = jax.nn.sigmoid(ma) * (oa @ w_pa) + jax.nn.sigmoid(mb) * (ob @ w_pb)
    return y @ w_o


def setup_inputs(seed: int = 0) -> dict:
    key = jax.random.key(seed)
    ks = iter(jax.random.split(key, 40))
    f32 = jnp.float32

    def nrm(shape, scale):
        return jax.random.normal(next(ks), shape, f32) * scale

    def gain(shape):
        return 1.0 + nrm(shape, 0.02)

    n_pages = PAST_LEN // PAGE_SIZE
    n_phys = (DEC_BATCH * n_pages * 5) // 4
    page_table = jax.random.permutation(next(ks), n_phys)[:DEC_BATCH * n_pages]
    page_table = page_table.reshape(DEC_BATCH, n_pages).astype(jnp.int32)
    return {
        'x_prompt': nrm((BATCH, SEQ, D_MODEL), 1.0),
        'x_sample': nrm((DEC_BATCH, DEC_SEQ, D_MODEL), 1.0),
        'cache_k': nrm((DEPTH, n_phys, PAGE_SIZE, A_HEADS, 2, A_QK), 1.0),
        'cache_v': nrm((DEPTH, n_phys, PAGE_SIZE, A_HEADS, A_V), 1.0),
        'state_gla': nrm((DEPTH, DEC_BATCH, B_HEADS, B_DK, B_DV), 0.3),
        'page_table': page_table,
        'ffn1_norm': gain((DEPTH, D_MODEL)),
        'ffn1_wg': nrm((DEPTH, D_MODEL, D_FF), D_MODEL ** -0.5),
        'ffn1_wu': nrm((DEPTH, D_MODEL, D_FF), D_MODEL ** -0.5),
        'ffn1_wd': nrm((DEPTH, D_FF, D_MODEL), D_FF ** -0.5),
        'mix_norm': gain((DEPTH, D_MODEL)),
        'w_in': nrm((DEPTH, D_MODEL, IN_W), D_MODEL ** -0.5),
        'w_gate_up': nrm((DEPTH, GATE_RANK, B_KW), GATE_RANK ** -0.5),
        'b_gate': nrm((DEPTH, B_KW), 0.1) + 1.0,
        'lam_q': nrm((DEPTH, 2, A_QK), 0.1),
        'lam_k': nrm((DEPTH, 2, A_QK), 0.1),
        'a_subln': gain((DEPTH, A_V)),
        'b_norm': gain((DEPTH, B_DV)),
        'w_pa': nrm((DEPTH, A_W, D_MODEL), A_W ** -0.5),
        'w_pb': nrm((DEPTH, B_VW, D_MODEL), B_VW ** -0.5),
        'w_o': nrm((DEPTH, D_MODEL, D_MODEL), D_MODEL ** -0.5),
        'ffn2_norm': gain((DEPTH, D_MODEL)),
        'ffn2_wg': nrm((DEPTH, D_MODEL, D_FF), D_MODEL ** -0.5),
        'ffn2_wu': nrm((DEPTH, D_MODEL, D_FF), D_MODEL ** -0.5),
        'ffn2_wd': nrm((DEPTH, D_FF, D_MODEL), D_FF ** -0.5),
        'final_norm': gain((D_MODEL,)),
    }


def reference(x_prompt, x_sample, cache_k, cache_v, state_gla, page_table,
              ffn1_norm, ffn1_wg, ffn1_wu, ffn1_wd, mix_norm, w_in, w_gate_up, b_gate,
              lam_q, lam_k, a_subln, b_norm, w_pa, w_pb, w_o,
              ffn2_norm, ffn2_wg, ffn2_wu, ffn2_wd, final_norm):
    xp, xs = x_prompt, x_sample
    kp_rows, vp_rows, sp_list, ks_rows, vs_rows, ss_list = [], [], [], [], [], []
    for l in range(DEPTH):
        xp = xp + macaron_half(xp, ffn1_norm[l], ffn1_wg[l], ffn1_wu[l], ffn1_wd[l])
        xs = xs + macaron_half(xs, ffn1_norm[l], ffn1_wg[l], ffn1_wu[l], ffn1_wd[l])

        lam_init = 0.8 - 0.6 * math.exp(-0.3 * l)
        lq = lam_q[l].astype(jnp.float32)
        lk = lam_k[l].astype(jnp.float32)
        lam = jnp.exp(jnp.sum(lq[0] * lk[0])) - jnp.exp(jnp.sum(lq[1] * lk[1])) + lam_init

        hp = rmsnorm(xp, mix_norm[l])
        qa, ka, va, qb, kb, vb, la, r, ma, mb = branch_inputs(hp, w_in[l], w_gate_up[l], b_gate[l])
        oa = diff_attn_prompt(qa, ka, va, lam)
        s0 = jnp.zeros((xp.shape[0], B_HEADS, B_DK, B_DV), jnp.float32)
        ob, sfin = gla(qb, kb, vb, la, s0)
        xp = xp + merge(oa, ob, r, ma, mb, lam_init, a_subln[l], b_norm[l], w_pa[l], w_pb[l], w_o[l], xp.dtype)
        kp_rows.append(ka)
        vp_rows.append(va)
        sp_list.append(sfin.astype(state_gla.dtype))

        hs = rmsnorm(xs, mix_norm[l])
        qa, ka, va, qb, kb, vb, la, r, ma, mb = branch_inputs(hs, w_in[l], w_gate_up[l], b_gate[l])
        oa = diff_attn_sample(qa, ka, va, cache_k, cache_v, page_table, l, lam)
        ob, sfin = gla(qb, kb, vb, la, state_gla[l])
        xs = xs + merge(oa, ob, r, ma, mb, lam_init, a_subln[l], b_norm[l], w_pa[l], w_pb[l], w_o[l], xs.dtype)
        ks_rows.append(ka)
        vs_rows.append(va)
        ss_list.append(sfin.astype(state_gla.dtype))

        xp = xp + macaron_half(xp, ffn2_norm[l], ffn2_wg[l], ffn2_wu[l], ffn2_wd[l])
        xs = xs + macaron_half(xs, ffn2_norm[l], ffn2_wg[l], ffn2_wu[l], ffn2_wd[l])

    y_prompt = rmsnorm(xp, final_norm)
    y_sample = rmsnorm(xs, final_norm)
    k_prompt = jnp.stack(kp_rows)
    v_prompt = jnp.stack(vp_rows)
    gla_prompt = jnp.stack(sp_list)
    k_sample = jnp.stack(ks_rows)
    v_sample = jnp.stack(vs_rows)
    gla_sample = jnp.stack(ss_list)
    return (y_prompt, y_sample, k_prompt, v_prompt, gla_prompt, k_sample, v_sample, gla_sample)
```

```python
import functools
import math

import jax
import jax.numpy as jnp
from jax import lax
from jax.experimental import pallas as pl
from jax.experimental.pallas import tpu as pltpu

EPS = 1e-6
GATE_TAU = 16.0
F32 = jnp.float32
BF16 = jnp.bfloat16
NEG = -1e30

LANES = 128
VMEM_LIMIT = 56 * 1024 * 1024


def _dot(a, b):
    return jnp.dot(a, b, preferred_element_type=F32)


def _dot_nt(a, b):
    return lax.dot_general(a, b, (((1,), (1,)), ((), ())), preferred_element_type=F32)


def _dot_tn(a, b):
    return lax.dot_general(a, b, (((0,), (0,)), ((), ())), preferred_element_type=F32)


def _rms(x, g):
    return x * lax.rsqrt(jnp.mean(x * x, axis=-1, keepdims=True) + EPS) * g


def _sigmoid(x):
    return 1.0 / (1.0 + jnp.exp(-x))


def _row_tile(m, target):
    best = None
    for t in range(8, min(m, target) + 1, 8):
        if m % t == 0:
            best = t
    assert best is not None, m
    return best


def _chunks(n, size):
    out, s = [], 0
    while s < n:
        out.append((s, min(size, n - s)))
        s += size
    return out


def _const_spec(shape):
    nd = len(shape)
    return pl.BlockSpec(shape, lambda *_: (0,) * nd, pipeline_mode=pl.Buffered(1))


def _params(sem):
    return pltpu.CompilerParams(dimension_semantics=sem, vmem_limit_bytes=VMEM_LIMIT)


def _ffn_kernel(*refs, f_chunks, has_merge, has_final):
    it = iter(refs)
    x_ref = next(it)
    if has_merge:
        oa_ref, ob_ref, sma_ref, smb_ref, wpa_ref, wpb_ref, wo_ref = (next(it) for _ in range(7))
    g_ref, wg_ref, wu_ref, wd_ref = (next(it) for _ in range(4))
    if has_final:
        gf_ref = next(it)
    o_ref = next(it)

    x = x_ref[...]
    if has_merge:
        ya = _dot(oa_ref[...].astype(BF16), wpa_ref[...])
        yb = _dot(ob_ref[...].astype(BF16), wpb_ref[...])
        y = sma_ref[...].astype(F32) * ya + smb_ref[...].astype(F32) * yb
        x = x + _dot(y.astype(BF16), wo_ref[...])
    h = _rms(x, g_ref[...]).astype(BF16)
    acc = None
    for s, n in f_chunks:
        gate = _dot(h, wg_ref[:, s:s + n])
        up = _dot(h, wu_ref[:, s:s + n])
        act = (gate * _sigmoid(gate) * up).astype(BF16)
        y = _dot(act, wd_ref[s:s + n, :])
        acc = y if acc is None else acc + y
    out = x + 0.5 * acc
    if has_final:
        out = _rms(out, gf_ref[...])
    o_ref[...] = out


def _ffn(x, g, wg, wu, wd, merge=None, final_g=None, tm_target=512):
    m, d = x.shape
    f = wg.shape[1]
    tm = _row_tile(m, tm_target)
    row = lambda w: pl.BlockSpec((tm, w), lambda i: (i, 0))
    args, specs = [x], [row(d)]
    if merge is not None:
        oa, ob, sma, smb, wpa, wpb, wo = merge
        args += [oa, ob, sma, smb, wpa, wpb, wo]
        specs += [row(oa.shape[1]), row(ob.shape[1]), row(d), row(d),
                  _const_spec(wpa.shape), _const_spec(wpb.shape), _const_spec(wo.shape)]
    args += [g.reshape(1, d), wg, wu, wd]
    specs += [_const_spec((1, d)), _const_spec(wg.shape), _const_spec(wu.shape), _const_spec(wd.shape)]
    if final_g is not None:
        args.append(final_g.reshape(1, d))
        specs.append(_const_spec((1, d)))
    kern = functools.partial(_ffn_kernel, f_chunks=_chunks(f, 1024),
                             has_merge=merge is not None, has_final=final_g is not None)
    return pl.pallas_call(
        kern, grid=(m // tm,), in_specs=specs, out_specs=row(d),
        out_shape=jax.ShapeDtypeStruct((m, d), F32),
        compiler_params=_params(("parallel",)), name="ffn",
    )(*args)


def _win_kernel(x_ref, g_ref, w_ref, wgu_ref, bg_ref,
                qa_ref, ka_ref, kab_ref, va_ref, vab_ref, qb_ref, kb_ref, vb_ref,
                la_ref, sr_ref, sma_ref, smb_ref, *, aw, bkw, bvw, d, qk_scale_a, qk_scale_b):
    h = _rms(x_ref[...], g_ref[...]).astype(BF16)
    o = 0
    p = _dot(h, w_ref[:, o:o + 3 * aw]); o += 3 * aw
    qa_ref[...] = p[:, :aw] * qk_scale_a
    ka = p[:, aw:2 * aw]
    va = p[:, 2 * aw:3 * aw]
    ka_ref[...] = ka
    kab_ref[...] = ka.astype(BF16)
    va_ref[...] = va
    vab_ref[...] = va.astype(BF16)
    p = _dot(h, w_ref[:, o:o + 2 * bkw + bvw]); o += 2 * bkw + bvw
    qb_ref[...] = p[:, :bkw] * qk_scale_b
    kb_ref[...] = p[:, bkw:2 * bkw]
    vb_ref[...] = p[:, 2 * bkw:].astype(vb_ref.dtype)
    p = _dot(h, w_ref[:, o:o + bvw]); o += bvw
    sr_ref[...] = (p * _sigmoid(p)).astype(sr_ref.dtype)
    p = _dot(h, w_ref[:, o:o + d]); o += d
    sma_ref[...] = _sigmoid(p).astype(sma_ref.dtype)
    p = _dot(h, w_ref[:, o:o + d]); o += d
    smb_ref[...] = _sigmoid(p).astype(smb_ref.dtype)
    gdown = _dot(h, w_ref[:, o:o + LANES])
    z = jnp.dot(gdown, wgu_ref[...], precision=lax.Precision.HIGHEST,
                preferred_element_type=F32) + bg_ref[...]
    la_ref[...] = (jnp.minimum(z, 0.0) - jnp.log(1.0 + jnp.exp(-jnp.abs(z)))) * (1.0 / GATE_TAU)


def _win(x, g, w_perm, wgu_pad, bg, dims, narrow, tm_target=512):
    m, d = x.shape
    aw, bkw, bvw, qk_a, qk_b = dims
    tm = _row_tile(m, tm_target)
    nd = BF16 if narrow else F32
    row = lambda w: pl.BlockSpec((tm, w), lambda i: (i, 0))
    outs = [(aw, F32), (aw, F32), (aw, BF16), (aw, F32), (aw, BF16), (bkw, F32), (bkw, F32), (bvw, nd),
            (bkw, F32), (bvw, nd), (d, BF16), (d, BF16)]
    kern = functools.partial(_win_kernel, aw=aw, bkw=bkw, bvw=bvw, d=d,
                             qk_scale_a=qk_a ** -0.5, qk_scale_b=qk_b ** -0.5)
    return pl.pallas_call(
        kern, grid=(m // tm,),
        in_specs=[row(d), _const_spec((1, d)), _const_spec(w_perm.shape), _const_spec(wgu_pad.shape),
                  _const_spec((1, bkw))],
        out_specs=[row(w) for w, _ in outs],
        out_shape=[jax.ShapeDtypeStruct((m, w), t) for w, t in outs],
        compiler_params=_params(("parallel",)), name="w_in",
    )(x, g.reshape(1, d), w_perm, wgu_pad, bg.reshape(1, bkw))


def _lam(lq_ref, lk_ref, lam_init):
    lq = lq_ref[...]
    lk = lk_ref[...]
    s = jnp.sum(lq * lk, axis=-1, keepdims=True)
    e = jnp.exp(s)
    return e[0:1, :] - e[1:2, :] + lam_init


def _attn_prompt_kernel(q_ref, k_ref, v_ref, lq_ref, lk_ref, g_ref, o_ref, acc1_ref, acc2_ref,
                        *, t, qk, lam_init):
    i = pl.program_id(2)
    q = q_ref[...].astype(BF16)
    lane = lax.broadcasted_iota(jnp.int32, q.shape, 1)
    q1 = jnp.where(lane < qk, q, jnp.zeros_like(q))
    q2 = jnp.where(lane >= qk, q, jnp.zeros_like(q))
    acc1_ref[...] = jnp.zeros_like(acc1_ref)
    acc2_ref[...] = jnp.zeros_like(acc2_ref)

    def update(s, m, l, acc_ref, vc):
        m_new = jnp.maximum(m, jnp.max(s, axis=-1, keepdims=True))
        alpha = jnp.exp(m - m_new)
        p = jnp.exp(s - m_new)
        l_new = alpha * l + jnp.sum(p, axis=-1, keepdims=True)
        acc_ref[...] = alpha * acc_ref[...] + _dot(p.astype(BF16), vc)
        return m_new, l_new

    def step(j, carry, masked):
        m1, l1, m2, l2 = carry
        off = pl.multiple_of(j * t, t)
        kc = k_ref[pl.ds(off, t), :]
        vc = v_ref[pl.ds(off, t), :]
        s1 = _dot_nt(q1, kc)
        s2 = _dot_nt(q2, kc)
        if masked:
            r = lax.broadcasted_iota(jnp.int32, s1.shape, 0)
            c = lax.broadcasted_iota(jnp.int32, s1.shape, 1)
            keep = c <= r
            s1 = jnp.where(keep, s1, NEG)
            s2 = jnp.where(keep, s2, NEG)
        m1, l1 = update(s1, m1, l1, acc1_ref, vc)
        m2, l2 = update(s2, m2, l2, acc2_ref, vc)
        return m1, l1, m2, l2

    init = (jnp.full((t, 1), NEG, F32), jnp.zeros((t, 1), F32),
            jnp.full((t, 1), NEG, F32), jnp.zeros((t, 1), F32))
    carry = lax.fori_loop(0, i, lambda j, c: step(j, c, False), init)
    _, l1, _, l2 = step(i, carry, True)
    lam = _lam(lq_ref, lk_ref, lam_init)
    o = acc1_ref[...] / l1 - lam * (acc2_ref[...] / l2)
    o_ref[...] = _rms(o, g_ref[...]).astype(o_ref.dtype)


def _attn_prompt(qa, kab, vab, lq, lk, g_eff, batch, seq, heads, qk, lam_init, t_target=512):
    mp, aw = qa.shape
    hw = aw // heads
    t = _row_tile(seq, t_target)
    nq = seq // t
    kern = functools.partial(_attn_prompt_kernel, t=t, qk=qk, lam_init=lam_init)
    return pl.pallas_call(
        kern, grid=(batch, heads, nq),
        in_specs=[pl.BlockSpec((t, hw), lambda b, h, i: (b * nq + i, h)),
                  pl.BlockSpec((seq, hw), lambda b, h, i: (b, h)),
                  pl.BlockSpec((seq, hw), lambda b, h, i: (b, h)),
                  pl.BlockSpec((2, qk), lambda b, h, i: (0, 0)),
                  pl.BlockSpec((2, qk), lambda b, h, i: (0, 0)),
                  pl.BlockSpec((1, hw), lambda b, h, i: (0, 0))],
        out_specs=pl.BlockSpec((t, hw), lambda b, h, i: (b * nq + i, h)),
        out_shape=jax.ShapeDtypeStruct((mp, aw), BF16),
        scratch_shapes=[pltpu.VMEM((t, hw), F32), pltpu.VMEM((t, hw), F32)],
        compiler_params=_params(("parallel", "parallel", "arbitrary")), name="attn_prompt",
    )(qa, kab, vab, lq, lk, g_eff.reshape(1, hw))


def _attn_sample_kernel(pt_ref, q_ref, kn_ref, vn_ref, lq_ref, lk_ref, g_ref, *rest,
                        pp, l, heads, qk, page, lam_init):
    kc_refs = rest[:pp]
    vc_refs = rest[pp:2 * pp]
    o_ref = rest[2 * pp]
    qbd_ref, m_ref, l_ref, acc_ref = rest[2 * pp + 1:]
    j = pl.program_id(1)
    hm = 2 * heads
    hw = 2 * qk

    def online(s_list, v_list, m, lsum, acc):
        s = jnp.concatenate(s_list, axis=1) if len(s_list) > 1 else s_list[0]
        m_new = jnp.maximum(m, jnp.max(s, axis=-1, keepdims=True))
        alpha = jnp.exp(m - m_new)
        p = jnp.exp(s - m_new)
        l_new = alpha * lsum + jnp.sum(p, axis=-1, keepdims=True)
        acc = alpha * acc
        for n, v in enumerate(v_list):
            acc = acc + _dot(p[:, n * page:(n + 1) * page].astype(BF16), v)
        return m_new, l_new, acc

    @pl.when(j == 0)
    def _():
        qt = jnp.concatenate([q_ref[...]] * hm, axis=0)
        rg = lax.broadcasted_iota(jnp.int32, qt.shape, 0) // l
        lg = lax.broadcasted_iota(jnp.int32, qt.shape, 1) // qk
        qbd = jnp.where(rg == lg, qt, 0.0).astype(BF16)
        qbd_ref[...] = qbd
        pad = jnp.zeros((page - l, heads * hw), F32)
        kn = jnp.concatenate([kn_ref[...], pad], axis=0).astype(BF16)
        vn = jnp.concatenate([vn_ref[...], pad], axis=0).astype(BF16)
        s = _dot_nt(qbd, kn)
        qi = lax.broadcasted_iota(jnp.int32, s.shape, 0) % l
        kj = lax.broadcasted_iota(jnp.int32, s.shape, 1)
        s = jnp.where(kj <= qi, s, NEG)
        m0 = jnp.full((hm * l, 1), NEG, F32)
        z0 = jnp.zeros((hm * l, 1), F32)
        m, lsum, acc = online([s], [vn], m0, z0, jnp.zeros((hm * l, heads * hw), F32))
        m_ref[...] = m
        l_ref[...] = lsum
        acc_ref[...] = acc

    qbd = qbd_ref[...]
    s_list = [_dot_nt(qbd, kc[...].astype(BF16)) for kc in kc_refs]
    v_list = [vc[...].astype(BF16) for vc in vc_refs]
    m, lsum, acc = online(s_list, v_list, m_ref[...], l_ref[...], acc_ref[...])
    m_ref[...] = m
    l_ref[...] = lsum
    acc_ref[...] = acc

    @pl.when(j == pl.num_programs(1) - 1)
    def _():
        lam = _lam(lq_ref, lk_ref, lam_init)
        a = acc_ref[...] / l_ref[...]
        outs = []
        for h in range(heads):
            a1 = a[(2 * h) * l:(2 * h + 1) * l, h * hw:(h + 1) * hw]
            a2 = a[(2 * h + 1) * l:(2 * h + 2) * l, h * hw:(h + 1) * hw]
            outs.append(_rms(a1 - lam * a2, g_ref[...]))
        o_ref[...] = jnp.concatenate(outs, axis=1).astype(o_ref.dtype)


def _attn_sample(qa, ka, va, lq, lk, g_eff, cache_k, cache_v, page_table, layer, dec_batch, l, heads, qk,
                 lam_init, pp_target=8):
    ms, aw = qa.shape
    hw = aw // heads
    page = cache_k.shape[2]
    n_pages = page_table.shape[1]
    pp = max(p for p in range(1, pp_target + 1) if n_pages % p == 0)
    assert l % 8 == 0 and l <= page
    hm = 2 * heads
    kern = functools.partial(_attn_sample_kernel, pp=pp, l=l, heads=heads, qk=qk, page=page, lam_init=lam_init)
    rows = pl.BlockSpec((l, aw), lambda b, j, pt: (b, 0))
    small = lambda shape: pl.BlockSpec(shape, lambda b, j, pt: (0, 0))

    def page_spec(n):
        return pl.BlockSpec((None, None, page, aw),
                            lambda b, j, pt: (layer, pt[b * n_pages + j * pp + n], 0, 0))

    grid_spec = pltpu.PrefetchScalarGridSpec(
        num_scalar_prefetch=1, grid=(dec_batch, n_pages // pp),
        in_specs=[rows, rows, rows, small((2, qk)), small((2, qk)), small((1, hw))]
                 + [page_spec(n) for n in range(pp)] + [page_spec(n) for n in range(pp)],
        out_specs=rows,
        scratch_shapes=[pltpu.VMEM((hm * l, aw), BF16), pltpu.VMEM((hm * l, 1), F32),
                        pltpu.VMEM((hm * l, 1), F32), pltpu.VMEM((hm * l, aw), F32)])
    return pl.pallas_call(
        kern, grid_spec=grid_spec, out_shape=jax.ShapeDtypeStruct((ms, aw), F32),
        compiler_params=_params(("parallel", "arbitrary")), name="attn_sample",
    )(page_table.reshape(-1), qa, ka, va, lq, lk, g_eff.reshape(1, hw),
      *([cache_k] * pp), *([cache_v] * pp))


def _gla_kernel(q_ref, k_ref, v_ref, la_ref, sr_ref, bn_ref, s0_ref, o_ref, sf_ref, st_ref,
                *, rows, chunk, sub, heads, dk, dv):
    j = pl.program_id(1)
    kw = heads * dk
    vw = heads * dv

    bd = (lax.broadcasted_iota(jnp.int32, (vw, kw), 0) // dv
          == lax.broadcasted_iota(jnp.int32, (vw, kw), 1) // dk)

    @pl.when(j == 0)
    def _():
        st_ref[...] = jnp.zeros_like(st_ref)
        for h in range(heads):
            st_ref[h * dv:(h + 1) * dv, h * dk:(h + 1) * dk] = s0_ref[h].T

    tri = (lax.broadcasted_iota(jnp.int32, (chunk, chunk), 0)
           >= lax.broadcasted_iota(jnp.int32, (chunk, chunk), 1)).astype(F32)
    lane_head = lax.broadcasted_iota(jnp.int32, (sub, kw), 1) // dk

    for c0 in range(0, rows, chunk):
        qc = q_ref[c0:c0 + chunk, :]
        kc = k_ref[c0:c0 + chunk, :]
        vc = v_ref[c0:c0 + chunk, :].astype(BF16)
        b = jnp.dot(tri, la_ref[c0:c0 + chunk, :], precision=lax.Precision.HIGHEST,
                    preferred_element_type=F32)
        b_last = b[chunk - 1:chunk, :]
        st = st_ref[...]
        o = _dot_nt((qc * jnp.exp(b)).astype(BF16), st.astype(BF16))
        kd = (kc * jnp.exp(b_last - b)).astype(BF16)
        st_ref[...] = st * jnp.exp(b_last) + jnp.where(bd, _dot_tn(vc, kd), 0.0)
        parts = []
        for lo in range(0, chunk, sub):
            hi = lo + sub
            ref = b[lo - 1:lo, :] if lo > 0 else jnp.zeros((1, kw), F32)
            qi = qc[lo:hi, :] * jnp.exp(b[lo:hi, :] - ref)
            qs = jnp.concatenate([jnp.where(lane_head == h, qi, 0.0) for h in range(heads)],
                                 axis=0).astype(BF16)
            ki = (kc[:hi, :] * jnp.exp(ref - b[:hi, :])).astype(BF16)
            att = _dot_nt(qs, ki)
            tt = lax.broadcasted_iota(jnp.int32, att.shape, 0) % sub + lo
            ss = lax.broadcasted_iota(jnp.int32, att.shape, 1)
            att = jnp.where(ss <= tt, att, 0.0).astype(BF16)
            ov = _dot(att, vc[:hi, :])
            parts.append(jnp.concatenate(
                [ov[h * sub:(h + 1) * sub, h * dv:(h + 1) * dv] for h in range(heads)], axis=1))
        o = o + (jnp.concatenate(parts, axis=0) if len(parts) > 1 else parts[0])
        sr = sr_ref[c0:c0 + chunk, :].astype(F32)
        outs = [_rms(o[:, h * dv:(h + 1) * dv], bn_ref[...]) * sr[:, h * dv:(h + 1) * dv]
                for h in range(heads)]
        o_ref[c0:c0 + chunk, :] = jnp.concatenate(outs, axis=1).astype(o_ref.dtype)

    @pl.when(j == pl.num_programs(1) - 1)
    def _():
        for h in range(heads):
            sf_ref[h] = st_ref[h * dv:(h + 1) * dv, h * dk:(h + 1) * dk].T


def _gla(qb, kb, vb, la, sr, bn, s0, batch, seq, out_dtype, chunk_target=64, rows_target=256):
    m, kw = qb.shape
    vw = vb.shape[1]
    _, heads, dk, dv = s0.shape
    chunk = min(chunk_target, seq)
    assert seq % chunk == 0 and chunk % 8 == 0
    sub = min(16, chunk)
    assert chunk % sub == 0
    rows = chunk * max(1, min(rows_target // chunk, seq // chunk))
    while seq % rows:
        rows -= chunk
    nj = seq // rows
    kern = functools.partial(_gla_kernel, rows=rows, chunk=chunk, sub=sub, heads=heads, dk=dk, dv=dv)
    row = lambda w: pl.BlockSpec((rows, w), lambda b, j: (b * nj + j, 0))
    state = pl.BlockSpec((None, heads, dk, dv), lambda b, j: (b, 0, 0, 0))
    return pl.pallas_call(
        kern, grid=(batch, nj),
        in_specs=[row(kw), row(kw), row(vw), row(kw), row(vw),
                  pl.BlockSpec((1, dv), lambda b, j: (0, 0)), state],
        out_specs=[row(vw), state],
        out_shape=[jax.ShapeDtypeStruct((m, vw), out_dtype), jax.ShapeDtypeStruct(s0.shape, F32)],
        scratch_shapes=[pltpu.VMEM((vw, kw), F32)],
        compiler_params=_params(("parallel", "arbitrary")), name="gla",
    )(qb, kb, vb, la, sr, bn.reshape(1, dv), s0)


def kernel(x_prompt, x_sample, cache_k, cache_v, state_gla, page_table,
           ffn1_norm, ffn1_wg, ffn1_wu, ffn1_wd, mix_norm, w_in, w_gate_up, b_gate,
           lam_q, lam_k, a_subln, b_norm, w_pa, w_pb, w_o,
           ffn2_norm, ffn2_wg, ffn2_wu, ffn2_wd, final_norm):
    batch, seq, d = x_prompt.shape
    dec_batch, dec_seq, _ = x_sample.shape
    depth, n_phys, page, a_heads, _, a_qk = cache_k.shape
    a_v = cache_v.shape[-1]
    _, _, b_heads, b_dk, b_dv = state_gla.shape
    rank = w_gate_up.shape[1]
    aw, bkw, bvw = a_heads * a_v, b_heads * b_dk, b_heads * b_dv
    assert a_v == 2 * a_qk and aw == a_heads * 2 * a_qk

    o_gd = 3 * aw + 2 * bkw + bvw
    w_perm = jnp.concatenate(
        [w_in[:, :, :o_gd], w_in[:, :, o_gd + rank:], w_in[:, :, o_gd:o_gd + rank],
         jnp.zeros((depth, d, LANES - rank), w_in.dtype)], axis=-1).astype(BF16)
    wgu_pad = jnp.concatenate([w_gate_up, jnp.zeros((depth, LANES - rank, bkw), F32)], axis=1)
    bf = lambda w: w.astype(BF16)
    f1g, f1u, f1d = bf(ffn1_wg), bf(ffn1_wu), bf(ffn1_wd)
    f2g, f2u, f2d = bf(ffn2_wg), bf(ffn2_wu), bf(ffn2_wd)
    wpa, wpb, wo = bf(w_pa), bf(w_pb), bf(w_o)
    ck = cache_k.reshape(depth, n_phys, page, aw)
    cv = cache_v.reshape(depth, n_phys, page, aw)

    xp = x_prompt.reshape(batch * seq, d)
    xs = x_sample.reshape(dec_batch * dec_seq, d)
    s0_prompt = jnp.zeros((batch, b_heads, b_dk, b_dv), F32)
    dims = (aw, bkw, bvw, a_qk, b_dk)
    kp, vp, sp, ks, vs, ss = [], [], [], [], [], []
    for l in range(depth):
        lam_init = 0.8 - 0.6 * math.exp(-0.3 * l)
        g_eff = a_subln[l] * (1.0 - lam_init)
        last = l == depth - 1

        xp = _ffn(xp, ffn1_norm[l], f1g[l], f1u[l], f1d[l])
        xs = _ffn(xs, ffn1_norm[l], f1g[l], f1u[l], f1d[l])

        qa, ka, kab, va, vab, qb, kb, vb, la, sr, sma, smb = _win(
            xp, mix_norm[l], w_perm[l], wgu_pad[l], b_gate[l], dims, narrow=True)
        oa = _attn_prompt(qa, kab, vab, lam_q[l], lam_k[l], g_eff, batch, seq, a_heads, a_qk, lam_init)
        ob, sfin = _gla(qb, kb, vb, la, sr, b_norm[l], s0_prompt, batch, seq, BF16)
        xp = _ffn(xp, ffn2_norm[l], f2g[l], f2u[l], f2d[l],
                  merge=(oa, ob, sma, smb, wpa[l], wpb[l], wo[l]), final_g=final_norm if last else None)
        kp.append(ka); vp.append(va); sp.append(sfin)

        qa, ka, kab, va, vab, qb, kb, vb, la, sr, sma, smb = _win(
            xs, mix_norm[l], w_perm[l], wgu_pad[l], b_gate[l], dims, narrow=False)
        oa = _attn_sample(qa, ka, va, lam_q[l], lam_k[l], g_eff, ck, cv, page_table, l,
                          dec_batch, dec_seq, a_heads, a_qk, lam_init)
        ob, sfin = _gla(qb, kb, vb, la, sr, b_norm[l], state_gla[l], dec_batch, dec_seq, F32)
        xs = _ffn(xs, ffn2_norm[l], f2g[l], f2u[l], f2d[l],
                  merge=(oa, ob, sma, smb, wpa[l], wpb[l], wo[l]), final_g=final_norm if last else None)
        ks.append(ka); vs.append(va); ss.append(sfin)

    y_prompt = xp.reshape(batch, seq, d)
    y_sample = xs.reshape(dec_batch, dec_seq, d)
    k_prompt = jnp.stack(kp).reshape(depth, batch, seq, a_heads, 2, a_qk)
    v_prompt = jnp.stack(vp).reshape(depth, batch, seq, a_heads, a_v)
    gla_prompt = jnp.stack(sp)
    k_sample = jnp.stack(ks).reshape(depth, dec_batch, dec_seq, a_heads, 2, a_qk)
    v_sample = jnp.stack(vs).reshape(depth, dec_batch, dec_seq, a_heads, a_v)
    gla_sample = jnp.stack(ss)
    return (y_prompt, y_sample, k_prompt, v_prompt, gla_prompt, k_sample, v_sample, gla_sample)
```

```python
import functools
import math

import jax
import jax.numpy as jnp
from jax import lax
from jax.experimental import pallas as pl
from jax.experimental.pallas import tpu as pltpu

EPS = 1e-6
GATE_TAU = 16.0
F32 = jnp.float32
BF16 = jnp.bfloat16
NEG = -1e30

LANES = 128
VMEM_LIMIT = 56 * 1024 * 1024


def _dot(a, b):
    return jnp.dot(a, b, preferred_element_type=F32)


def _dot_nt(a, b):
    return lax.dot_general(a, b, (((1,), (1,)), ((), ())), preferred_element_type=F32)


def _dot_tn(a, b):
    return lax.dot_general(a, b, (((0,), (0,)), ((), ())), preferred_element_type=F32)


def _rms(x, g):
    return x * lax.rsqrt(jnp.mean(x * x, axis=-1, keepdims=True) + EPS) * g


def _sigmoid(x):
    return 1.0 / (1.0 + jnp.exp(-x))


def _row_tile(m, target):
    best = None
    for t in range(8, min(m, target) + 1, 8):
        if m % t == 0:
            best = t
    assert best is not None, m
    return best


def _chunks(n, size):
    out, s = [], 0
    while s < n:
        out.append((s, min(size, n - s)))
        s += size
    return out


def _const_spec(shape):
    nd = len(shape)
    return pl.BlockSpec(shape, lambda *_: (0,) * nd, pipeline_mode=pl.Buffered(1))


def _params(sem):
    return pltpu.CompilerParams(dimension_semantics=sem, vmem_limit_bytes=VMEM_LIMIT)


def _ffn_kernel(*refs, f_chunks, has_merge, has_final):
    it = iter(refs)
    x_ref = next(it)
    if has_merge:
        oa_ref, ob_ref, sma_ref, smb_ref, wpa_ref, wpb_ref, wo_ref = (next(it) for _ in range(7))
    g_ref, wg_ref, wu_ref, wd_ref = (next(it) for _ in range(4))
    if has_final:
        gf_ref = next(it)
    o_ref = next(it)

    x = x_ref[...]
    if has_merge:
        ya = _dot(oa_ref[...].astype(BF16), wpa_ref[...])
        yb = _dot(ob_ref[...].astype(BF16), wpb_ref[...])
        y = sma_ref[...].astype(F32) * ya + smb_ref[...].astype(F32) * yb
        x = x + _dot(y.astype(BF16), wo_ref[...])
    h = _rms(x, g_ref[...]).astype(BF16)
    acc = None
    for s, n in f_chunks:
        gate = _dot(h, wg_ref[:, s:s + n])
        up = _dot(h, wu_ref[:, s:s + n])
        act = (gate * _sigmoid(gate) * up).astype(BF16)
        y = _dot(act, wd_ref[s:s + n, :])
        acc = y if acc is None else acc + y
    out = x + 0.5 * acc
    if has_final:
        out = _rms(out, gf_ref[...])
    o_ref[...] = out


def _ffn(x, g, wg, wu, wd, merge=None, final_g=None, tm_target=512):
    m, d = x.shape
    f = wg.shape[1]
    tm = _row_tile(m, tm_target)
    row = lambda w: pl.BlockSpec((tm, w), lambda i: (i, 0))
    args, specs = [x], [row(d)]
    if merge is not None:
        oa, ob, sma, smb, wpa, wpb, wo = merge
        args += [oa, ob, sma, smb, wpa, wpb, wo]
        specs += [row(oa.shape[1]), row(ob.shape[1]), row(d), row(d),
                  _const_spec(wpa.shape), _const_spec(wpb.shape), _const_spec(wo.shape)]
    args += [g.reshape(1, d), wg, wu, wd]
    specs += [_const_spec((1, d)), _const_spec(wg.shape), _const_spec(wu.shape), _const_spec(wd.shape)]
    if final_g is not None:
        args.append(final_g.reshape(1, d))
        specs.append(_const_spec((1, d)))
    kern = functools.partial(_ffn_kernel, f_chunks=_chunks(f, 1024),
                             has_merge=merge is not None, has_final=final_g is not None)
    return pl.pallas_call(
        kern, grid=(m // tm,), in_specs=specs, out_specs=row(d),
        out_shape=jax.ShapeDtypeStruct((m, d), F32),
        compiler_params=_params(("parallel",)), name="ffn",
    )(*args)


def _win_kernel(x_ref, g_ref, w_ref, wgu_ref, bg_ref,
                qa_ref, ka_ref, kab_ref, va_ref, vab_ref, qb_ref, kb_ref, vb_ref,
                la_ref, sr_ref, sma_ref, smb_ref, *, aw, bkw, bvw, d, qk_scale_a, qk_scale_b):
    h = _rms(x_ref[...], g_ref[...]).astype(BF16)
    o = 0
    p = _dot(h, w_ref[:, o:o + 3 * aw]); o += 3 * aw
    qa_ref[...] = p[:, :aw] * qk_scale_a
    ka = p[:, aw:2 * aw]
    va = p[:, 2 * aw:3 * aw]
    ka_ref[...] = ka
    kab_ref[...] = ka.astype(BF16)
    va_ref[...] = va
    vab_ref[...] = va.astype(BF16)
    p = _dot(h, w_ref[:, o:o + 2 * bkw + bvw]); o += 2 * bkw + bvw
    qb_ref[...] = p[:, :bkw] * qk_scale_b
    kb_ref[...] = p[:, bkw:2 * bkw]
    vb_ref[...] = p[:, 2 * bkw:].astype(vb_ref.dtype)
    p = _dot(h, w_ref[:, o:o + bvw]); o += bvw
    sr_ref[...] = (p * _sigmoid(p)).astype(sr_ref.dtype)
    p = _dot(h, w_ref[:, o:o + d]); o += d
    sma_ref[...] = _sigmoid(p).astype(sma_ref.dtype)
    p = _dot(h, w_ref[:, o:o + d]); o += d
    smb_ref[...] = _sigmoid(p).astype(smb_ref.dtype)
    gdown = _dot(h, w_ref[:, o:o + LANES])
    z = jnp.dot(gdown, wgu_ref[...], precision=lax.Precision.HIGHEST,
                preferred_element_type=F32) + bg_ref[...]
    la_ref[...] = (jnp.minimum(z, 0.0) - jnp.log(1.0 + jnp.exp(-jnp.abs(z)))) * (1.0 / GATE_TAU)


def _win(x, g, w_perm, wgu_pad, bg, dims, narrow, tm_target=512):
    m, d = x.shape
    aw, bkw, bvw, qk_a, qk_b = dims
    tm = _row_tile(m, tm_target)
    nd = BF16 if narrow else F32
    row = lambda w: pl.BlockSpec((tm, w), lambda i: (i, 0))
    outs = [(aw, F32), (aw, F32), (aw, BF16), (aw, F32), (aw, BF16), (bkw, F32), (bkw, F32), (bvw, nd),
            (bkw, F32), (bvw, nd), (d, BF16), (d, BF16)]
    kern = functools.partial(_win_kernel, aw=aw, bkw=bkw, bvw=bvw, d=d,
                             qk_scale_a=qk_a ** -0.5, qk_scale_b=qk_b ** -0.5)
    return pl.pallas_call(
        kern, grid=(m // tm,),
        in_specs=[row(d), _const_spec((1, d)), _const_spec(w_perm.shape), _const_spec(wgu_pad.shape),
                  _const_spec((1, bkw))],
        out_specs=[row(w) for w, _ in outs],
        out_shape=[jax.ShapeDtypeStruct((m, w), t) for w, t in outs],
        compiler_params=_params(("parallel",)), name="w_in",
    )(x, g.reshape(1, d), w_perm, wgu_pad, bg.reshape(1, bkw))


def _lam(lq_ref, lk_ref, lam_init):
    lq = lq_ref[...]
    lk = lk_ref[...]
    s = jnp.sum(lq * lk, axis=-1, keepdims=True)
    e = jnp.exp(s)
    return e[0:1, :] - e[1:2, :] + lam_init


def _attn_prompt_kernel(q_ref, k_ref, v_ref, lq_ref, lk_ref, g_ref, o_ref, acc_ref, ml_ref, sa_ref, sb_ref,
                        *, t, kr, qk, nh, lam_init):
    i = pl.program_id(2)
    hw = 2 * qk
    qcat = []
    for h in range(nh):
        q = q_ref[:, h * hw:(h + 1) * hw] * math.log2(math.e)
        lane = lax.broadcasted_iota(jnp.int32, q.shape, 1)
        qcat.append(jnp.concatenate([jnp.where(lane < qk, q, 0.0), jnp.where(lane >= qk, q, 0.0)],
                                    axis=0).astype(BF16))
    acc_ref[...] = jnp.zeros_like(acc_ref)

    tk = kr * t

    def scores_to(s_ref, j):
        off = pl.multiple_of(j * tk, tk)
        for h in range(nh):
            s_ref[h] = _dot_nt(k_ref[pl.ds(off, tk), h * hw:(h + 1) * hw], qcat[h])

    def consume(j, s_ref, masked):
        off = pl.multiple_of(j * tk, tk)
        ps, alphas = [], []
        for h in range(nh):
            m = ml_ref[h, 0:1, :]
            l = ml_ref[h, 1:2, :]
            s = s_ref[h]
            if masked:
                d = (lax.broadcasted_iota(jnp.int32, s.shape, 0)
                     - lax.broadcasted_iota(jnp.int32, s.shape, 1) % t)
                s = jnp.where(d <= i * t - off, s, NEG)
            m_new = jnp.maximum(m, jnp.max(s, axis=0, keepdims=True))
            alpha = jnp.exp2(m - m_new)
            p = jnp.exp2(s - m_new)
            ml_ref[h, 0:1, :] = m_new
            ml_ref[h, 1:2, :] = alpha * l + jnp.sum(p, axis=0, keepdims=True)
            ps.append(p.astype(BF16))
            alphas.append(alpha)
        pvs = [_dot_tn(v_ref[pl.ds(off, tk), h * hw:(h + 1) * hw], ps[h]) for h in range(nh)]
        for h in range(nh):
            acc_ref[h] = alphas[h] * acc_ref[h] + pvs[h]

    nfull = i // kr
    for h in range(nh):
        ml_ref[h, 0:1, :] = jnp.full((1, 2 * t), NEG, F32)
        ml_ref[h, 1:2, :] = jnp.zeros((1, 2 * t), F32)
    scores_to(sa_ref, 0)

    def body(jj, carry):
        j = 2 * jj
        scores_to(sb_ref, j + 1)
        consume(j, sa_ref, False)
        scores_to(sa_ref, j + 2)
        consume(j + 1, sb_ref, False)
        return carry

    npairs = nfull // 2
    lax.fori_loop(0, npairs, body, 0)

    @pl.when(nfull % 2 == 1)
    def _():
        scores_to(sb_ref, nfull)
        consume(nfull - 1, sa_ref, False)
        consume(nfull, sb_ref, True)

    @pl.when(nfull % 2 == 0)
    def _():
        consume(nfull, sa_ref, True)

    lam = _lam(lq_ref, lk_ref, lam_init)
    for h in range(nh):
        a = acc_ref[h] / ml_ref[h, 1:2, :]
        o = (a[:, :t] - lam * a[:, t:]).T
        o_ref[:, h * hw:(h + 1) * hw] = _rms(o, g_ref[...]).astype(o_ref.dtype)


def _attn_prompt(qa, kab, vab, lq, lk, g_eff, batch, seq, heads, qk, lam_init, t_target=256, kr_target=2,
                 nh_target=4):
    mp, aw = qa.shape
    hw = aw // heads
    t = _row_tile(seq, t_target)
    nq = seq // t
    kr = max(r for r in range(1, kr_target + 1) if nq % r == 0)
    nh = max(n for n in range(1, nh_target + 1) if heads % n == 0)
    kern = functools.partial(_attn_prompt_kernel, t=t, kr=kr, qk=qk, nh=nh, lam_init=lam_init)
    kv_spec = pl.BlockSpec((seq, nh * hw), lambda b, h, i: (b, h), pipeline_mode=pl.Buffered(1))
    return pl.pallas_call(
        kern, grid=(batch, heads // nh, nq),
        in_specs=[pl.BlockSpec((t, nh * hw), lambda b, h, i: (b * nq + i, h)),
                  kv_spec, kv_spec,
                  pl.BlockSpec((2, qk), lambda b, h, i: (0, 0)),
                  pl.BlockSpec((2, qk), lambda b, h, i: (0, 0)),
                  pl.BlockSpec((1, hw), lambda b, h, i: (0, 0))],
        out_specs=pl.BlockSpec((t, nh * hw), lambda b, h, i: (b * nq + i, h)),
        out_shape=jax.ShapeDtypeStruct((mp, aw), BF16),
        scratch_shapes=[pltpu.VMEM((nh, hw, 2 * t), F32), pltpu.VMEM((nh, 8, 2 * t), F32),
                        pltpu.VMEM((nh, kr * t, 2 * t), F32), pltpu.VMEM((nh, kr * t, 2 * t), F32)],
        compiler_params=_params(("parallel", "parallel", "arbitrary")), name="attn_prompt",
    )(qa, kab, vab, lq, lk, g_eff.reshape(1, hw))


def _attn_sample_kernel(pt_ref, q_ref, kn_ref, vn_ref, lq_ref, lk_ref, g_ref, *rest,
                        pp, l, heads, qk, page, lam_init):
    kc_refs = rest[:pp]
    vc_refs = rest[pp:2 * pp]
    o_ref = rest[2 * pp]
    qbd_ref, m_ref, l_ref, acc_ref = rest[2 * pp + 1:]
    j = pl.program_id(1)
    hm = 2 * heads
    hw = 2 * qk

    def online(s, v_heads, m, lsum, acc):
        m_new = jnp.maximum(m, jnp.max(s, axis=-1, keepdims=True))
        alpha = jnp.exp(m - m_new)
        p = jnp.exp(s - m_new)
        l_new = alpha * lsum + jnp.sum(p, axis=-1, keepdims=True)
        pv = [_dot(p[2 * h * l:(2 * h + 2) * l, :].astype(BF16), v_heads[h]) for h in range(heads)]
        return m_new, l_new, alpha * acc + jnp.concatenate(pv, axis=0)

    @pl.when(j == 0)
    def _():
        qt = jnp.concatenate([q_ref[...]] * hm, axis=0)
        rg = lax.broadcasted_iota(jnp.int32, qt.shape, 0) // l
        lg = lax.broadcasted_iota(jnp.int32, qt.shape, 1) // qk
        qbd = jnp.where(rg == lg, qt, 0.0).astype(BF16)
        qbd_ref[...] = qbd
        pad = jnp.zeros((page - l, heads * hw), F32)
        kn = jnp.concatenate([kn_ref[...], pad], axis=0).astype(BF16)
        vn = jnp.concatenate([vn_ref[...], pad], axis=0).astype(BF16)
        s = _dot_nt(qbd, kn)
        qi = lax.broadcasted_iota(jnp.int32, s.shape, 0) % l
        kj = lax.broadcasted_iota(jnp.int32, s.shape, 1)
        s = jnp.where(kj <= qi, s, NEG)
        m0 = jnp.full((hm * l, 1), NEG, F32)
        z0 = jnp.zeros((hm * l, 1), F32)
        m, lsum, acc = online(s, [vn[:, h * hw:(h + 1) * hw] for h in range(heads)],
                              m0, z0, jnp.zeros((hm * l, hw), F32))
        m_ref[...] = m
        l_ref[...] = lsum
        acc_ref[...] = acc

    qbd = qbd_ref[...]
    s_list = [_dot(qbd, kc[...].astype(BF16)) for kc in kc_refs]
    s = jnp.concatenate(s_list, axis=1) if pp > 1 else s_list[0]
    v_heads = []
    for h in range(heads):
        vh = [vc[pl.ds(h, page, stride=heads), :] for vc in vc_refs]
        v_heads.append((jnp.concatenate(vh, axis=0) if pp > 1 else vh[0]).astype(BF16))
    m, lsum, acc = online(s, v_heads, m_ref[...], l_ref[...], acc_ref[...])
    m_ref[...] = m
    l_ref[...] = lsum
    acc_ref[...] = acc

    @pl.when(j == pl.num_programs(1) - 1)
    def _():
        lam = _lam(lq_ref, lk_ref, lam_init)
        a = acc_ref[...] / l_ref[...]
        outs = [_rms(a[2 * h * l:(2 * h + 1) * l, :] - lam * a[(2 * h + 1) * l:(2 * h + 2) * l, :], g_ref[...])
                for h in range(heads)]
        o_ref[...] = jnp.concatenate(outs, axis=1).astype(o_ref.dtype)


def _attn_sample(qa, ka, va, lq, lk, g_eff, cache_kt, cache_vr, page_table, layer, dec_batch, l, heads, qk,
                 lam_init, pp_target=8):
    ms, aw = qa.shape
    hw = aw // heads
    page = cache_kt.shape[3]
    n_pages = page_table.shape[1]
    pp = max(p for p in range(1, pp_target + 1) if n_pages % p == 0)
    assert l % 8 == 0 and l <= page
    hm = 2 * heads
    kern = functools.partial(_attn_sample_kernel, pp=pp, l=l, heads=heads, qk=qk, page=page, lam_init=lam_init)
    rows = pl.BlockSpec((l, aw), lambda b, j, pt: (b, 0))
    small = lambda shape: pl.BlockSpec(shape, lambda b, j, pt: (0, 0))

    def page_spec(n, shape):
        return pl.BlockSpec((None, None) + shape,
                            lambda b, j, pt: (layer, pt[b * n_pages + j * pp + n], 0, 0))

    grid_spec = pltpu.PrefetchScalarGridSpec(
        num_scalar_prefetch=1, grid=(dec_batch, n_pages // pp),
        in_specs=[rows, rows, rows, small((2, qk)), small((2, qk)), small((1, hw))]
                 + [page_spec(n, (aw, page)) for n in range(pp)]
                 + [page_spec(n, (page * heads, hw)) for n in range(pp)],
        out_specs=rows,
        scratch_shapes=[pltpu.VMEM((hm * l, aw), BF16), pltpu.VMEM((hm * l, 1), F32),
                        pltpu.VMEM((hm * l, 1), F32), pltpu.VMEM((hm * l, hw), F32)])
    return pl.pallas_call(
        kern, grid_spec=grid_spec, out_shape=jax.ShapeDtypeStruct((ms, aw), F32),
        compiler_params=_params(("parallel", "arbitrary")), name="attn_sample",
    )(page_table.reshape(-1), qa, ka, va, lq, lk, g_eff.reshape(1, hw),
      *([cache_kt] * pp), *([cache_vr] * pp))


def _gla_kernel(q_ref, k_ref, v_ref, la_ref, sr_ref, bn_ref, s0_ref, o_ref, sf_ref, st_ref,
                *, rows, chunk, sub, heads, dk, dv):
    j = pl.program_id(1)
    kw = heads * dk
    vw = heads * dv

    @pl.when(j == 0)
    def _():
        for h in range(heads):
            st_ref[:, h * dk:(h + 1) * dk] = s0_ref[h].T

    tri = (lax.broadcasted_iota(jnp.int32, (chunk, chunk), 0)
           >= lax.broadcasted_iota(jnp.int32, (chunk, chunk), 1)).astype(F32)
    head_of = lambda n: lax.broadcasted_iota(jnp.int32, (n, kw), 1) // dk

    def stack_heads(x):
        lh = head_of(x.shape[0])
        return jnp.concatenate([jnp.where(lh == h, x, 0.0) for h in range(heads)], axis=0).astype(BF16)

    starts = list(range(0, rows, chunk))
    subs = list(range(0, chunk, sub))
    qcs = [q_ref[c0:c0 + chunk, :] for c0 in starts]
    kcs = [k_ref[c0:c0 + chunk, :] for c0 in starts]
    vcs = [v_ref[c0:c0 + chunk, :].astype(BF16) for c0 in starts]
    bs = [jnp.dot(tri, la_ref[c0:c0 + chunk, :], precision=lax.Precision.HIGHEST,
                  preferred_element_type=F32) for c0 in starts]
    qsts, kds, decs, qss, kis = [], [], [], [], []
    for qc, kc, b in zip(qcs, kcs, bs):
        b_last = b[chunk - 1:chunk, :]
        qsts.append(stack_heads(qc * jnp.exp(b)))
        kds.append((kc * jnp.exp(b_last - b)).astype(BF16))
        decs.append(jnp.exp(b_last))
        for lo in subs:
            hi = lo + sub
            ref = b[lo - 1:lo, :] if lo > 0 else jnp.zeros((1, kw), F32)
            qss.append(stack_heads(qc[lo:hi, :] * jnp.exp(b[lo:hi, :] - ref)))
            kis.append((kc[:hi, :] * jnp.exp(ref - b[:hi, :])).astype(BF16))
    fulls = [_dot_tn(vc, kd) for vc, kd in zip(vcs, kds)]
    atts = [_dot_nt(qs, ki) for qs, ki in zip(qss, kis)]
    lh = head_of(dv)
    upds = [sum(jnp.where(lh == h, full[h * dv:(h + 1) * dv, :], 0.0) for h in range(heads)) for full in fulls]
    masked = []
    for n, att in enumerate(atts):
        lo = subs[n % len(subs)]
        tt = lax.broadcasted_iota(jnp.int32, att.shape, 0) % sub + lo
        ss = lax.broadcasted_iota(jnp.int32, att.shape, 1)
        masked.append(jnp.where(ss <= tt, att, 0.0).astype(BF16))
    ovs = [_dot(att, vcs[n // len(subs)][:subs[n % len(subs)] + sub, :]) for n, att in enumerate(masked)]
    pre = []
    for c in range(len(starts)):
        parts = [jnp.concatenate([ov[h * sub:(h + 1) * sub, h * dv:(h + 1) * dv] for h in range(heads)], axis=1)
                 for ov in ovs[c * len(subs):(c + 1) * len(subs)]]
        intra = jnp.concatenate(parts, axis=0) if len(parts) > 1 else parts[0]
        pre.append((qsts[c], upds[c], decs[c], intra))

    st = st_ref[...]
    for n, c0 in enumerate(range(0, rows, chunk)):
        qst, upd, dec, intra = pre[n]
        r = _dot_nt(qst, st.astype(BF16))
        o = jnp.concatenate([r[h * chunk:(h + 1) * chunk, :] for h in range(heads)], axis=1) + intra
        st = st * dec + upd
        sr = sr_ref[c0:c0 + chunk, :].astype(F32)
        outs = [_rms(o[:, h * dv:(h + 1) * dv], bn_ref[...]) * sr[:, h * dv:(h + 1) * dv]
                for h in range(heads)]
        o_ref[c0:c0 + chunk, :] = jnp.concatenate(outs, axis=1).astype(o_ref.dtype)
    st_ref[...] = st

    @pl.when(j == pl.num_programs(1) - 1)
    def _():
        for h in range(heads):
            sf_ref[h] = st[:, h * dk:(h + 1) * dk].T


def _gla(qb, kb, vb, la, sr, bn, s0, batch, seq, out_dtype, chunk_target=64, rows_target=512):
    m, kw = qb.shape
    vw = vb.shape[1]
    _, heads, dk, dv = s0.shape
    chunk = min(chunk_target, seq)
    assert seq % chunk == 0 and chunk % 8 == 0
    sub = min(16, chunk)
    assert chunk % sub == 0
    rows = chunk * max(1, min(rows_target // chunk, seq // chunk))
    while seq % rows:
        rows -= chunk
    nj = seq // rows
    kern = functools.partial(_gla_kernel, rows=rows, chunk=chunk, sub=sub, heads=heads, dk=dk, dv=dv)
    row = lambda w: pl.BlockSpec((rows, w), lambda b, j: (b * nj + j, 0))
    state = pl.BlockSpec((None, heads, dk, dv), lambda b, j: (b, 0, 0, 0))
    return pl.pallas_call(
        kern, grid=(batch, nj),
        in_specs=[row(kw), row(kw), row(vw), row(kw), row(vw),
                  pl.BlockSpec((1, dv), lambda b, j: (0, 0)), state],
        out_specs=[row(vw), state],
        out_shape=[jax.ShapeDtypeStruct((m, vw), out_dtype), jax.ShapeDtypeStruct(s0.shape, F32)],
        scratch_shapes=[pltpu.VMEM((dv, kw), F32)],
        compiler_params=_params(("parallel", "arbitrary")), name="gla",
    )(qb, kb, vb, la, sr, bn.reshape(1, dv), s0)


def kernel(x_prompt, x_sample, cache_k, cache_v, state_gla, page_table,
           ffn1_norm, ffn1_wg, ffn1_wu, ffn1_wd, mix_norm, w_in, w_gate_up, b_gate,
           lam_q, lam_k, a_subln, b_norm, w_pa, w_pb, w_o,
           ffn2_norm, ffn2_wg, ffn2_wu, ffn2_wd, final_norm):
    batch, seq, d = x_prompt.shape
    dec_batch, dec_seq, _ = x_sample.shape
    depth, n_phys, page, a_heads, _, a_qk = cache_k.shape
    a_v = cache_v.shape[-1]
    _, _, b_heads, b_dk, b_dv = state_gla.shape
    rank = w_gate_up.shape[1]
    aw, bkw, bvw = a_heads * a_v, b_heads * b_dk, b_heads * b_dv
    assert a_v == 2 * a_qk and aw == a_heads * 2 * a_qk

    o_gd = 3 * aw + 2 * bkw + bvw
    w_perm = jnp.concatenate(
        [w_in[:, :, :o_gd], w_in[:, :, o_gd + rank:], w_in[:, :, o_gd:o_gd + rank],
         jnp.zeros((depth, d, LANES - rank), w_in.dtype)], axis=-1).astype(BF16)
    wgu_pad = jnp.concatenate([w_gate_up, jnp.zeros((depth, LANES - rank, bkw), F32)], axis=1)
    bf = lambda w: w.astype(BF16)
    f1g, f1u, f1d = bf(ffn1_wg), bf(ffn1_wu), bf(ffn1_wd)
    f2g, f2u, f2d = bf(ffn2_wg), bf(ffn2_wu), bf(ffn2_wd)
    wpa, wpb, wo = bf(w_pa), bf(w_pb), bf(w_o)
    ck = jnp.transpose(cache_k, (0, 1, 3, 4, 5, 2)).reshape(depth, n_phys, aw, page)
    cv = cache_v.reshape(depth, n_phys, page * a_heads, a_v)

    xp = x_prompt.reshape(batch * seq, d)
    xs = x_sample.reshape(dec_batch * dec_seq, d)
    s0_prompt = jnp.zeros((batch, b_heads, b_dk, b_dv), F32)
    dims = (aw, bkw, bvw, a_qk, b_dk)
    kp, vp, sp, ks, vs, ss = [], [], [], [], [], []
    for l in range(depth):
        lam_init = 0.8 - 0.6 * math.exp(-0.3 * l)
        g_eff = a_subln[l] * (1.0 - lam_init)
        last = l == depth - 1

        xp = _ffn(xp, ffn1_norm[l], f1g[l], f1u[l], f1d[l])
        xs = _ffn(xs, ffn1_norm[l], f1g[l], f1u[l], f1d[l])

        qa, ka, kab, va, vab, qb, kb, vb, la, sr, sma, smb = _win(
            xp, mix_norm[l], w_perm[l], wgu_pad[l], b_gate[l], dims, narrow=True)
        oa = _attn_prompt(qa, kab, vab, lam_q[l], lam_k[l], g_eff, batch, seq, a_heads, a_qk, lam_init)
        ob, sfin = _gla(qb, kb, vb, la, sr, b_norm[l], s0_prompt, batch, seq, BF16)
        xp = _ffn(xp, ffn2_norm[l], f2g[l], f2u[l], f2d[l],
                  merge=(oa, ob, sma, smb, wpa[l], wpb[l], wo[l]), final_g=final_norm if last else None)
        kp.append(ka); vp.append(va); sp.append(sfin)

        qa, ka, kab, va, vab, qb, kb, vb, la, sr, sma, smb = _win(
            xs, mix_norm[l], w_perm[l], wgu_pad[l], b_gate[l], dims, narrow=False)
        oa = _attn_sample(qa, ka, va, lam_q[l], lam_k[l], g_eff, ck, cv, page_table, l,
                          dec_batch, dec_seq, a_heads, a_qk, lam_init)
        ob, sfin = _gla(qb, kb, vb, la, sr, b_norm[l], state_gla[l], dec_batch, dec_seq, F32)
        xs = _ffn(xs, ffn2_norm[l], f2g[l], f2u[l], f2d[l],
                  merge=(oa, ob, sma, smb, wpa[l], wpb[l], wo[l]), final_g=final_norm if last else None)
        ks.append(ka); vs.append(va); ss.append(sfin)

    y_prompt = xp.reshape(batch, seq, d)
    y_sample = xs.reshape(dec_batch, dec_seq, d)
    k_prompt = jnp.stack(kp).reshape(depth, batch, seq, a_heads, 2, a_qk)
    v_prompt = jnp.stack(vp).reshape(depth, batch, seq, a_heads, a_v)
    gla_prompt = jnp.stack(sp)
    k_sample = jnp.stack(ks).reshape(depth, dec_batch, dec_seq, a_heads, 2, a_qk)
    v_sample = jnp.stack(vs).reshape(depth, dec_batch, dec_seq, a_heads, a_v)
    gla_sample = jnp.stack(ss)
    return (y_prompt, y_sample, k_prompt, v_prompt, gla_prompt, k_sample, v_sample, gla_sample)
```

```python
import functools
import math

import jax
import jax.numpy as jnp
from jax import lax
from jax.experimental import pallas as pl
from jax.experimental.pallas import tpu as pltpu

EPS = 1e-6
GATE_TAU = 16.0
F32 = jnp.float32
BF16 = jnp.bfloat16
NEG = -1e30

LANES = 128
VMEM_LIMIT = 56 * 1024 * 1024


def _dot(a, b):
    return jnp.dot(a, b, preferred_element_type=F32)


def _dot_nt(a, b):
    return lax.dot_general(a, b, (((1,), (1,)), ((), ())), preferred_element_type=F32)


def _dot_tn(a, b):
    return lax.dot_general(a, b, (((0,), (0,)), ((), ())), preferred_element_type=F32)


def _rms(x, g):
    return x * lax.rsqrt(jnp.mean(x * x, axis=-1, keepdims=True) + EPS) * g


def _sigmoid(x):
    return 1.0 / (1.0 + jnp.exp(-x))


def _row_tile(m, target):
    best = None
    for t in range(8, min(m, target) + 1, 8):
        if m % t == 0:
            best = t
    assert best is not None, m
    return best


def _chunks(n, size):
    out, s = [], 0
    while s < n:
        out.append((s, min(size, n - s)))
        s += size
    return out


def _const_spec(shape):
    nd = len(shape)
    return pl.BlockSpec(shape, lambda *_: (0,) * nd, pipeline_mode=pl.Buffered(1))


def _params(sem):
    return pltpu.CompilerParams(dimension_semantics=sem, vmem_limit_bytes=VMEM_LIMIT)


def _ffn_kernel(*refs, f_chunks, has_merge, has_final):
    it = iter(refs)
    x_ref = next(it)
    if has_merge:
        oa_ref, ob_ref, sma_ref, smb_ref, wpa_ref, wpb_ref, wo_ref = (next(it) for _ in range(7))
    g_ref, wg_ref, wu_ref, wd_ref = (next(it) for _ in range(4))
    if has_final:
        gf_ref = next(it)
    o_ref = next(it)

    x = x_ref[...]
    if has_merge:
        ya = _dot(oa_ref[...].astype(BF16), wpa_ref[...])
        yb = _dot(ob_ref[...].astype(BF16), wpb_ref[...])
        y = sma_ref[...].astype(F32) * ya + smb_ref[...].astype(F32) * yb
        x = x + _dot(y.astype(BF16), wo_ref[...])
    h = _rms(x, g_ref[...]).astype(BF16)
    acc = None
    for s, n in f_chunks:
        gate = _dot(h, wg_ref[:, s:s + n])
        up = _dot(h, wu_ref[:, s:s + n])
        act = (gate * _sigmoid(gate) * up).astype(BF16)
        y = _dot(act, wd_ref[s:s + n, :])
        acc = y if acc is None else acc + y
    out = x + 0.5 * acc
    if has_final:
        out = _rms(out, gf_ref[...])
    o_ref[...] = out


def _ffn(x, g, wg, wu, wd, merge=None, final_g=None, tm_target=512):
    m, d = x.shape
    f = wg.shape[1]
    tm = _row_tile(m, tm_target)
    row = lambda w: pl.BlockSpec((tm, w), lambda i: (i, 0))
    args, specs = [x], [row(d)]
    if merge is not None:
        oa, ob, sma, smb, wpa, wpb, wo = merge
        args += [oa, ob, sma, smb, wpa, wpb, wo]
        specs += [row(oa.shape[1]), row(ob.shape[1]), row(d), row(d),
                  _const_spec(wpa.shape), _const_spec(wpb.shape), _const_spec(wo.shape)]
    args += [g.reshape(1, d), wg, wu, wd]
    specs += [_const_spec((1, d)), _const_spec(wg.shape), _const_spec(wu.shape), _const_spec(wd.shape)]
    if final_g is not None:
        args.append(final_g.reshape(1, d))
        specs.append(_const_spec((1, d)))
    kern = functools.partial(_ffn_kernel, f_chunks=_chunks(f, 1024),
                             has_merge=merge is not None, has_final=final_g is not None)
    return pl.pallas_call(
        kern, grid=(m // tm,), in_specs=specs, out_specs=row(d),
        out_shape=jax.ShapeDtypeStruct((m, d), F32),
        compiler_params=_params(("parallel",)), name="ffn",
    )(*args)


def _win_kernel(x_ref, g_ref, wa_ref, wb_ref, wgd_ref, wgu_ref, bg_ref,
                qa_ref, ka_ref, kab_ref, va_ref, vab_ref, qb_ref, kb_ref, vb_ref,
                la_ref, sr_ref, sma_ref, smb_ref, *, aw, bkw, bvw, d, heads, native_kv, qk_scale_a, qk_scale_b):
    h = _rms(x_ref[...], g_ref[...]).astype(BF16)
    p = _dot(h, wa_ref[:, :3 * aw])
    qa_ref[...] = p[:, :aw] * qk_scale_a
    ka = p[:, aw:2 * aw]
    va = p[:, 2 * aw:3 * aw]
    if native_kv:
        ka_ref[...] = ka.T
        hv = aw // heads
        for hd in range(heads):
            va_ref[pl.ds(hd, ka.shape[0], stride=heads), :] = va[:, hd * hv:(hd + 1) * hv]
    else:
        ka_ref[...] = ka
        va_ref[...] = va
    kab_ref[...] = ka.astype(BF16)
    vab_ref[...] = va.astype(BF16)
    p = _dot(h, wa_ref[:, 3 * aw:])
    qb_ref[...] = p[:, :bkw] * qk_scale_b
    kb_ref[...] = p[:, bkw:2 * bkw]
    vb_ref[...] = p[:, 2 * bkw:].astype(vb_ref.dtype)
    p = _dot(h, wb_ref[:, :bvw])
    sr_ref[...] = (p * _sigmoid(p)).astype(sr_ref.dtype)
    p = _dot(h, wb_ref[:, bvw:bvw + d])
    sma_ref[...] = _sigmoid(p).astype(sma_ref.dtype)
    p = _dot(h, wb_ref[:, bvw + d:])
    smb_ref[...] = _sigmoid(p).astype(smb_ref.dtype)
    gdown = _dot(h, wgd_ref[...])
    z = jnp.dot(gdown, wgu_ref[...], precision=lax.Precision.HIGHEST,
                preferred_element_type=F32) + bg_ref[...]
    la_ref[...] = (jnp.minimum(z, 0.0) - jnp.log(1.0 + jnp.exp(-jnp.abs(z)))) * (1.0 / GATE_TAU)


def _win(x, g, w_parts, wgu_pad, bg, dims, narrow, seq=None, tm_target=512):
    m, d = x.shape
    aw, bkw, bvw, qk_a, qk_b, heads = dims
    native_kv = seq is not None
    tm = _row_tile(seq if native_kv else m, tm_target)
    nd = BF16 if narrow else F32
    row = lambda w: pl.BlockSpec((tm, w), lambda i: (i, 0))
    outs = [(aw, F32), (aw, F32), (aw, BF16), (aw, F32), (aw, BF16), (bkw, F32), (bkw, F32), (bvw, nd),
            (bkw, F32), (bvw, nd), (d, BF16), (d, BF16)]
    out_specs = [row(w) for w, _ in outs]
    out_shape = [jax.ShapeDtypeStruct((m, w), t) for w, t in outs]
    if native_kv:
        nt = seq // tm
        out_specs[1] = pl.BlockSpec((aw, tm), lambda i: (i // nt, i % nt))
        out_shape[1] = jax.ShapeDtypeStruct((m // seq * aw, seq), F32)
        out_specs[3] = pl.BlockSpec((tm * heads, aw // heads), lambda i: (i, 0))
        out_shape[3] = jax.ShapeDtypeStruct((m * heads, aw // heads), F32)
    kern = functools.partial(_win_kernel, aw=aw, bkw=bkw, bvw=bvw, d=d, heads=heads, native_kv=native_kv,
                             qk_scale_a=qk_a ** -0.5, qk_scale_b=qk_b ** -0.5)
    return pl.pallas_call(
        kern, grid=(m // tm,),
        in_specs=[row(d), _const_spec((1, d))] + [_const_spec(w.shape) for w in w_parts]
                 + [_const_spec(wgu_pad.shape), _const_spec((1, bkw))],
        out_specs=out_specs, out_shape=out_shape,
        compiler_params=_params(("parallel",)), name="w_in",
    )(x, g.reshape(1, d), *w_parts, wgu_pad, bg.reshape(1, bkw))


def _lam(lq_ref, lk_ref, lam_init):
    lq = lq_ref[...]
    lk = lk_ref[...]
    s = jnp.sum(lq * lk, axis=-1, keepdims=True)
    e = jnp.exp(s)
    return e[0:1, :] - e[1:2, :] + lam_init


def _attn_prompt_kernel(q_ref, k_ref, v_ref, lq_ref, lk_ref, g_ref, o_ref, acc_ref, ml_ref, sa_ref, sb_ref,
                        *, t, kr, qk, nh, lam_init):
    i = pl.program_id(2)
    hw = 2 * qk
    qcat = []
    for h in range(nh):
        q = q_ref[:, h * hw:(h + 1) * hw] * math.log2(math.e)
        lane = lax.broadcasted_iota(jnp.int32, q.shape, 1)
        qcat.append(jnp.concatenate([jnp.where(lane < qk, q, 0.0), jnp.where(lane >= qk, q, 0.0)],
                                    axis=0).astype(BF16))
    acc_ref[...] = jnp.zeros_like(acc_ref)

    tk = kr * t

    def scores_to(s_ref, j):
        off = pl.multiple_of(j * tk, tk)
        for h in range(nh):
            s_ref[h] = _dot_nt(k_ref[pl.ds(off, tk), h * hw:(h + 1) * hw], qcat[h])

    def consume(j, s_ref, masked):
        off = pl.multiple_of(j * tk, tk)
        ps, alphas = [], []
        for h in range(nh):
            m = ml_ref[h, 0:1, :]
            l = ml_ref[h, 1:2, :]
            s = s_ref[h]
            if masked:
                d = (lax.broadcasted_iota(jnp.int32, s.shape, 0)
                     - lax.broadcasted_iota(jnp.int32, s.shape, 1) % t)
                s = jnp.where(d <= i * t - off, s, NEG)
            m_new = jnp.maximum(m, jnp.max(s, axis=0, keepdims=True))
            alpha = jnp.exp2(m - m_new)
            p = jnp.exp2(s - m_new)
            ml_ref[h, 0:1, :] = m_new
            ml_ref[h, 1:2, :] = alpha * l + jnp.sum(p, axis=0, keepdims=True)
            ps.append(p.astype(BF16))
            alphas.append(alpha)
        pvs = [_dot_tn(v_ref[pl.ds(off, tk), h * hw:(h + 1) * hw], ps[h]) for h in range(nh)]
        for h in range(nh):
            acc_ref[h] = alphas[h] * acc_ref[h] + pvs[h]

    nfull = i // kr
    for h in range(nh):
        ml_ref[h, 0:1, :] = jnp.full((1, 2 * t), NEG, F32)
        ml_ref[h, 1:2, :] = jnp.zeros((1, 2 * t), F32)
    scores_to(sa_ref, 0)

    def body(jj, carry):
        j = 2 * jj
        scores_to(sb_ref, j + 1)
        consume(j, sa_ref, False)
        scores_to(sa_ref, j + 2)
        consume(j + 1, sb_ref, False)
        return carry

    npairs = nfull // 2
    lax.fori_loop(0, npairs, body, 0)

    @pl.when(nfull % 2 == 1)
    def _():
        scores_to(sb_ref, nfull)
        consume(nfull - 1, sa_ref, False)
        consume(nfull, sb_ref, True)

    @pl.when(nfull % 2 == 0)
    def _():
        consume(nfull, sa_ref, True)

    lam = _lam(lq_ref, lk_ref, lam_init)
    for h in range(nh):
        a = acc_ref[h] / ml_ref[h, 1:2, :]
        o = (a[:, :t] - lam * a[:, t:]).T
        o_ref[:, h * hw:(h + 1) * hw] = _rms(o, g_ref[...]).astype(o_ref.dtype)


def _attn_prompt(qa, kab, vab, lq, lk, g_eff, batch, seq, heads, qk, lam_init, t_target=256, kr_target=2,
                 nh_target=4):
    mp, aw = qa.shape
    hw = aw // heads
    t = _row_tile(seq, t_target)
    nq = seq // t
    kr = max(r for r in range(1, kr_target + 1) if nq % r == 0)
    nh = max(n for n in range(1, nh_target + 1) if heads % n == 0)
    kern = functools.partial(_attn_prompt_kernel, t=t, kr=kr, qk=qk, nh=nh, lam_init=lam_init)
    kv_spec = pl.BlockSpec((seq, nh * hw), lambda b, h, i: (b, h), pipeline_mode=pl.Buffered(1))
    return pl.pallas_call(
        kern, grid=(batch, heads // nh, nq),
        in_specs=[pl.BlockSpec((t, nh * hw), lambda b, h, i: (b * nq + i, h)),
                  kv_spec, kv_spec,
                  pl.BlockSpec((2, qk), lambda b, h, i: (0, 0)),
                  pl.BlockSpec((2, qk), lambda b, h, i: (0, 0)),
                  pl.BlockSpec((1, hw), lambda b, h, i: (0, 0))],
        out_specs=pl.BlockSpec((t, nh * hw), lambda b, h, i: (b * nq + i, h)),
        out_shape=jax.ShapeDtypeStruct((mp, aw), BF16),
        scratch_shapes=[pltpu.VMEM((nh, hw, 2 * t), F32), pltpu.VMEM((nh, 8, 2 * t), F32),
                        pltpu.VMEM((nh, kr * t, 2 * t), F32), pltpu.VMEM((nh, kr * t, 2 * t), F32)],
        compiler_params=_params(("parallel", "parallel", "arbitrary")), name="attn_prompt",
    )(qa, kab, vab, lq, lk, g_eff.reshape(1, hw))


def _attn_sample_kernel(pt_ref, q_ref, kn_ref, vn_ref, lq_ref, lk_ref, g_ref, *rest,
                        pp, l, heads, qk, page, lam_init):
    kc_refs = rest[:pp]
    vc_refs = rest[pp:2 * pp]
    o_ref = rest[2 * pp]
    qbd_ref, m_ref, l_ref, acc_ref = rest[2 * pp + 1:]
    j = pl.program_id(1)
    hm = 2 * heads
    hw = 2 * qk

    def online(s, v_heads, m, lsum, acc):
        m_new = jnp.maximum(m, jnp.max(s, axis=-1, keepdims=True))
        alpha = jnp.exp(m - m_new)
        p = jnp.exp(s - m_new)
        l_new = alpha * lsum + jnp.sum(p, axis=-1, keepdims=True)
        pv = [_dot(p[2 * h * l:(2 * h + 2) * l, :].astype(BF16), v_heads[h]) for h in range(heads)]
        return m_new, l_new, alpha * acc + jnp.concatenate(pv, axis=0)

    @pl.when(j == 0)
    def _():
        qt = jnp.concatenate([q_ref[...]] * hm, axis=0)
        rg = lax.broadcasted_iota(jnp.int32, qt.shape, 0) // l
        lg = lax.broadcasted_iota(jnp.int32, qt.shape, 1) // qk
        qbd = jnp.where(rg == lg, qt, 0.0).astype(BF16)
        qbd_ref[...] = qbd
        pad = jnp.zeros((page - l, heads * hw), F32)
        kn = jnp.concatenate([kn_ref[...], pad], axis=0).astype(BF16)
        vn = jnp.concatenate([vn_ref[...], pad], axis=0).astype(BF16)
        s = _dot_nt(qbd, kn)
        qi = lax.broadcasted_iota(jnp.int32, s.shape, 0) % l
        kj = lax.broadcasted_iota(jnp.int32, s.shape, 1)
        s = jnp.where(kj <= qi, s, NEG)
        m0 = jnp.full((hm * l, 1), NEG, F32)
        z0 = jnp.zeros((hm * l, 1), F32)
        m, lsum, acc = online(s, [vn[:, h * hw:(h + 1) * hw] for h in range(heads)],
                              m0, z0, jnp.zeros((hm * l, hw), F32))
        m_ref[...] = m
        l_ref[...] = lsum
        acc_ref[...] = acc

    qbd = qbd_ref[...]
    s_list = [_dot(qbd, kc[...].astype(BF16)) for kc in kc_refs]
    s = jnp.concatenate(s_list, axis=1) if pp > 1 else s_list[0]
    v_heads = []
    for h in range(heads):
        vh = [vc[pl.ds(h, page, stride=heads), :] for vc in vc_refs]
        v_heads.append((jnp.concatenate(vh, axis=0) if pp > 1 else vh[0]).astype(BF16))
    m, lsum, acc = online(s, v_heads, m_ref[...], l_ref[...], acc_ref[...])
    m_ref[...] = m
    l_ref[...] = lsum
    acc_ref[...] = acc

    @pl.when(j == pl.num_programs(1) - 1)
    def _():
        lam = _lam(lq_ref, lk_ref, lam_init)
        a = acc_ref[...] / l_ref[...]
        outs = [_rms(a[2 * h * l:(2 * h + 1) * l, :] - lam * a[(2 * h + 1) * l:(2 * h + 2) * l, :], g_ref[...])
                for h in range(heads)]
        o_ref[...] = jnp.concatenate(outs, axis=1).astype(o_ref.dtype)


def _attn_sample(qa, ka, va, lq, lk, g_eff, cache_kt, cache_vr, page_table, layer, dec_batch, l, heads, qk,
                 lam_init, pp_target=16):
    ms, aw = qa.shape
    hw = aw // heads
    page = cache_kt.shape[3]
    n_pages = page_table.shape[1]
    pp = max(p for p in range(1, pp_target + 1) if n_pages % p == 0)
    assert l % 8 == 0 and l <= page
    hm = 2 * heads
    kern = functools.partial(_attn_sample_kernel, pp=pp, l=l, heads=heads, qk=qk, page=page, lam_init=lam_init)
    rows = pl.BlockSpec((l, aw), lambda b, j, pt: (b, 0))
    small = lambda shape: pl.BlockSpec(shape, lambda b, j, pt: (0, 0))

    def page_spec(n, shape):
        return pl.BlockSpec((None, None) + shape,
                            lambda b, j, pt: (layer, pt[b * n_pages + j * pp + n], 0, 0))

    grid_spec = pltpu.PrefetchScalarGridSpec(
        num_scalar_prefetch=1, grid=(dec_batch, n_pages // pp),
        in_specs=[rows, rows, rows, small((2, qk)), small((2, qk)), small((1, hw))]
                 + [page_spec(n, (aw, page)) for n in range(pp)]
                 + [page_spec(n, (page * heads, hw)) for n in range(pp)],
        out_specs=rows,
        scratch_shapes=[pltpu.VMEM((hm * l, aw), BF16), pltpu.VMEM((hm * l, 1), F32),
                        pltpu.VMEM((hm * l, 1), F32), pltpu.VMEM((hm * l, hw), F32)])
    return pl.pallas_call(
        kern, grid_spec=grid_spec, out_shape=jax.ShapeDtypeStruct((ms, aw), F32),
        compiler_params=_params(("parallel", "arbitrary")), name="attn_sample",
    )(page_table.reshape(-1), qa, ka, va, lq, lk, g_eff.reshape(1, hw),
      *([cache_kt] * pp), *([cache_vr] * pp))


def _gla_kernel(q_ref, k_ref, v_ref, la_ref, sr_ref, bn_ref, s0_ref, o_ref, sf_ref, st_ref,
                *, rows, chunk, sub, heads, dk, dv):
    j = pl.program_id(1)
    kw = heads * dk
    vw = heads * dv

    @pl.when(j == 0)
    def _():
        for h in range(heads):
            st_ref[:, h * dk:(h + 1) * dk] = s0_ref[h].T

    tri = (lax.broadcasted_iota(jnp.int32, (chunk, chunk), 0)
           >= lax.broadcasted_iota(jnp.int32, (chunk, chunk), 1)).astype(F32)
    head_of = lambda n: lax.broadcasted_iota(jnp.int32, (n, kw), 1) // dk

    def stack_heads(x):
        lh = head_of(x.shape[0])
        return jnp.concatenate([jnp.where(lh == h, x, 0.0) for h in range(heads)], axis=0).astype(BF16)

    starts = list(range(0, rows, chunk))
    subs = list(range(0, chunk, sub))
    qcs = [q_ref[c0:c0 + chunk, :] for c0 in starts]
    kcs = [k_ref[c0:c0 + chunk, :] for c0 in starts]
    vcs = [v_ref[c0:c0 + chunk, :].astype(BF16) for c0 in starts]
    bs = [jnp.dot(tri, la_ref[c0:c0 + chunk, :], precision=lax.Precision.HIGHEST,
                  preferred_element_type=F32) for c0 in starts]
    qsts, kds, decs, qss, kis = [], [], [], [], []
    for qc, kc, b in zip(qcs, kcs, bs):
        b_last = b[chunk - 1:chunk, :]
        qsts.append(stack_heads(qc * jnp.exp(b)))
        kds.append((kc * jnp.exp(b_last - b)).astype(BF16))
        decs.append(jnp.exp(b_last))
        for lo in subs:
            hi = lo + sub
            ref = b[lo - 1:lo, :] if lo > 0 else jnp.zeros((1, kw), F32)
            qss.append(stack_heads(qc[lo:hi, :] * jnp.exp(b[lo:hi, :] - ref)))
            kis.append((kc[:hi, :] * jnp.exp(ref - b[:hi, :])).astype(BF16))
    fulls = [_dot_tn(vc, kd) for vc, kd in zip(vcs, kds)]
    atts = [_dot_nt(qs, ki) for qs, ki in zip(qss, kis)]
    lh = head_of(dv)
    upds = [sum(jnp.where(lh == h, full[h * dv:(h + 1) * dv, :], 0.0) for h in range(heads)) for full in fulls]
    masked = []
    for n, att in enumerate(atts):
        lo = subs[n % len(subs)]
        tt = lax.broadcasted_iota(jnp.int32, att.shape, 0) % sub + lo
        ss = lax.broadcasted_iota(jnp.int32, att.shape, 1)
        masked.append(jnp.where(ss <= tt, att, 0.0).astype(BF16))
    ovs = [_dot(att, vcs[n // len(subs)][:subs[n % len(subs)] + sub, :]) for n, att in enumerate(masked)]
    pre = []
    for c in range(len(starts)):
        parts = [jnp.concatenate([ov[h * sub:(h + 1) * sub, h * dv:(h + 1) * dv] for h in range(heads)], axis=1)
                 for ov in ovs[c * len(subs):(c + 1) * len(subs)]]
        intra = jnp.concatenate(parts, axis=0) if len(parts) > 1 else parts[0]
        pre.append((qsts[c], upds[c], decs[c], intra))

    st = st_ref[...]
    for n, c0 in enumerate(range(0, rows, chunk)):
        qst, upd, dec, intra = pre[n]
        r = _dot_nt(qst, st.astype(BF16))
        o = jnp.concatenate([r[h * chunk:(h + 1) * chunk, :] for h in range(heads)], axis=1) + intra
        st = st * dec + upd
        sr = sr_ref[c0:c0 + chunk, :].astype(F32)
        outs = [_rms(o[:, h * dv:(h + 1) * dv], bn_ref[...]) * sr[:, h * dv:(h + 1) * dv]
                for h in range(heads)]
        o_ref[c0:c0 + chunk, :] = jnp.concatenate(outs, axis=1).astype(o_ref.dtype)
    st_ref[...] = st

    @pl.when(j == pl.num_programs(1) - 1)
    def _():
        for h in range(heads):
            sf_ref[h] = st[:, h * dk:(h + 1) * dk].T


def _gla(qb, kb, vb, la, sr, bn, s0, batch, seq, out_dtype, chunk_target=64, rows_target=512):
    m, kw = qb.shape
    vw = vb.shape[1]
    _, heads, dk, dv = s0.shape
    chunk = min(chunk_target, seq)
    assert seq % chunk == 0 and chunk % 8 == 0
    sub = min(16, chunk)
    assert chunk % sub == 0
    rows = chunk * max(1, min(rows_target // chunk, seq // chunk))
    while seq % rows:
        rows -= chunk
    nj = seq // rows
    kern = functools.partial(_gla_kernel, rows=rows, chunk=chunk, sub=sub, heads=heads, dk=dk, dv=dv)
    row = lambda w: pl.BlockSpec((rows, w), lambda b, j: (b * nj + j, 0))
    state = pl.BlockSpec((None, heads, dk, dv), lambda b, j: (b, 0, 0, 0))
    return pl.pallas_call(
        kern, grid=(batch, nj),
        in_specs=[row(kw), row(kw), row(vw), row(kw), row(vw),
                  pl.BlockSpec((1, dv), lambda b, j: (0, 0)), state],
        out_specs=[row(vw), state],
        out_shape=[jax.ShapeDtypeStruct((m, vw), out_dtype), jax.ShapeDtypeStruct(s0.shape, F32)],
        scratch_shapes=[pltpu.VMEM((dv, kw), F32)],
        compiler_params=_params(("parallel", "arbitrary")), name="gla",
    )(qb, kb, vb, la, sr, bn.reshape(1, dv), s0)


def kernel(x_prompt, x_sample, cache_k, cache_v, state_gla, page_table,
           ffn1_norm, ffn1_wg, ffn1_wu, ffn1_wd, mix_norm, w_in, w_gate_up, b_gate,
           lam_q, lam_k, a_subln, b_norm, w_pa, w_pb, w_o,
           ffn2_norm, ffn2_wg, ffn2_wu, ffn2_wd, final_norm):
    batch, seq, d = x_prompt.shape
    dec_batch, dec_seq, _ = x_sample.shape
    depth, n_phys, page, a_heads, _, a_qk = cache_k.shape
    a_v = cache_v.shape[-1]
    _, _, b_heads, b_dk, b_dv = state_gla.shape
    rank = w_gate_up.shape[1]
    aw, bkw, bvw = a_heads * a_v, b_heads * b_dk, b_heads * b_dv
    assert a_v == 2 * a_qk and aw == a_heads * 2 * a_qk

    o_gd = 3 * aw + 2 * bkw + bvw
    w_a = w_in[:, :, :o_gd].astype(BF16)
    w_b = w_in[:, :, o_gd + rank:].astype(BF16)
    w_gd = jnp.pad(w_in[:, :, o_gd:o_gd + rank], ((0, 0), (0, 0), (0, LANES - rank))).astype(BF16)
    wgu_pad = jnp.concatenate([w_gate_up, jnp.zeros((depth, LANES - rank, bkw), F32)], axis=1)
    bf = lambda w: w.astype(BF16)
    f1g, f1u, f1d = bf(ffn1_wg), bf(ffn1_wu), bf(ffn1_wd)
    f2g, f2u, f2d = bf(ffn2_wg), bf(ffn2_wu), bf(ffn2_wd)
    wpa, wpb, wo = bf(w_pa), bf(w_pb), bf(w_o)
    ck = jnp.transpose(cache_k, (0, 1, 3, 4, 5, 2)).reshape(depth, n_phys, aw, page)
    cv = cache_v.reshape(depth, n_phys, page * a_heads, a_v)

    xp = x_prompt.reshape(batch * seq, d)
    xs = x_sample.reshape(dec_batch * dec_seq, d)
    s0_prompt = jnp.zeros((batch, b_heads, b_dk, b_dv), F32)
    dims = (aw, bkw, bvw, a_qk, b_dk, a_heads)
    kp, vp, sp, ks, vs, ss = [], [], [], [], [], []
    for l in range(depth):
        lam_init = 0.8 - 0.6 * math.exp(-0.3 * l)
        g_eff = a_subln[l] * (1.0 - lam_init)
        last = l == depth - 1

        xp = _ffn(xp, ffn1_norm[l], f1g[l], f1u[l], f1d[l])
        xs = _ffn(xs, ffn1_norm[l], f1g[l], f1u[l], f1d[l])

        qa, ka, kab, va, vab, qb, kb, vb, la, sr, sma, smb = _win(
            xp, mix_norm[l], (w_a[l], w_b[l], w_gd[l]), wgu_pad[l], b_gate[l], dims, narrow=True, seq=seq)
        oa = _attn_prompt(qa, kab, vab, lam_q[l], lam_k[l], g_eff, batch, seq, a_heads, a_qk, lam_init)
        ob, sfin = _gla(qb, kb, vb, la, sr, b_norm[l], s0_prompt, batch, seq, BF16)
        xp = _ffn(xp, ffn2_norm[l], f2g[l], f2u[l], f2d[l],
                  merge=(oa, ob, sma, smb, wpa[l], wpb[l], wo[l]), final_g=final_norm if last else None)
        kp.append(ka); vp.append(va); sp.append(sfin)

        qa, ka, kab, va, vab, qb, kb, vb, la, sr, sma, smb = _win(
            xs, mix_norm[l], (w_a[l], w_b[l], w_gd[l]), wgu_pad[l], b_gate[l], dims, narrow=False)
        oa = _attn_sample(qa, ka, va, lam_q[l], lam_k[l], g_eff, ck, cv, page_table, l,
                          dec_batch, dec_seq, a_heads, a_qk, lam_init)
        ob, sfin = _gla(qb, kb, vb, la, sr, b_norm[l], state_gla[l], dec_batch, dec_seq, F32)
        xs = _ffn(xs, ffn2_norm[l], f2g[l], f2u[l], f2d[l],
                  merge=(oa, ob, sma, smb, wpa[l], wpb[l], wo[l]), final_g=final_norm if last else None)
        ks.append(ka); vs.append(va); ss.append(sfin)

    y_prompt = xp.reshape(batch, seq, d)
    y_sample = xs.reshape(dec_batch, dec_seq, d)
    k_prompt = jnp.transpose(jnp.stack(kp).reshape(depth, batch, a_heads, 2, a_qk, seq), (0, 1, 5, 2, 3, 4))
    v_prompt = jnp.stack(vp).reshape(depth, batch, seq, a_heads, a_v)
    gla_prompt = jnp.stack(sp)
    k_sample = jnp.stack(ks).reshape(depth, dec_batch, dec_seq, a_heads, 2, a_qk)
    v_sample = jnp.stack(vs).reshape(depth, dec_batch, dec_seq, a_heads, a_v)
    gla_sample = jnp.stack(ss)
    return (y_prompt, y_sample, k_prompt, v_prompt, gla_prompt, k_sample, v_sample, gla_sample)
```

```python
import functools
import math

import jax
import jax.numpy as jnp
from jax import lax
from jax.experimental import pallas as pl
from jax.experimental.pallas import tpu as pltpu

EPS = 1e-6
GATE_TAU = 16.0
F32 = jnp.float32
BF16 = jnp.bfloat16
NEG = -1e30

LANES = 128
VMEM_LIMIT = 56 * 1024 * 1024


def _dot(a, b):
    return jnp.dot(a, b, preferred_element_type=F32)


def _dot_nt(a, b):
    return lax.dot_general(a, b, (((1,), (1,)), ((), ())), preferred_element_type=F32)


def _dot_tn(a, b):
    return lax.dot_general(a, b, (((0,), (0,)), ((), ())), preferred_element_type=F32)


def _rms(x, g):
    return x * lax.rsqrt(jnp.mean(x * x, axis=-1, keepdims=True) + EPS) * g


def _sigmoid(x):
    return 1.0 / (1.0 + jnp.exp(-x))


def _row_tile(m, target):
    best = None
    for t in range(8, min(m, target) + 1, 8):
        if m % t == 0:
            best = t
    assert best is not None, m
    return best


def _chunks(n, size):
    out, s = [], 0
    while s < n:
        out.append((s, min(size, n - s)))
        s += size
    return out


def _const_spec(shape):
    nd = len(shape)
    return pl.BlockSpec(shape, lambda *_: (0,) * nd, pipeline_mode=pl.Buffered(1))


def _params(sem):
    return pltpu.CompilerParams(dimension_semantics=sem, vmem_limit_bytes=VMEM_LIMIT)


def _ffn_kernel(*refs, f_chunks, has_merge, has_final):
    it = iter(refs)
    x_ref = next(it)
    if has_merge:
        oa_ref, ob_ref, sma_ref, smb_ref, wpa_ref, wpb_ref, wo_ref = (next(it) for _ in range(7))
    g_ref, wg_ref, wu_ref, wd_ref = (next(it) for _ in range(4))
    if has_final:
        gf_ref = next(it)
    o_ref = next(it)

    x = x_ref[...]
    if has_merge:
        ya = _dot(oa_ref[...].astype(BF16), wpa_ref[...])
        yb = _dot(ob_ref[...].astype(BF16), wpb_ref[...])
        y = sma_ref[...].astype(F32) * ya + smb_ref[...].astype(F32) * yb
        x = x + _dot(y.astype(BF16), wo_ref[...])
    h = _rms(x, g_ref[...]).astype(BF16)
    acc = None
    for s, n in f_chunks:
        gate = _dot(h, wg_ref[:, s:s + n])
        up = _dot(h, wu_ref[:, s:s + n])
        act = (gate * _sigmoid(gate) * up).astype(BF16)
        y = _dot(act, wd_ref[s:s + n, :])
        acc = y if acc is None else acc + y
    out = x + 0.5 * acc
    if has_final:
        out = _rms(out, gf_ref[...])
    o_ref[...] = out


def _ffn(x, g, wg, wu, wd, merge=None, final_g=None, tm_target=512):
    m, d = x.shape
    f = wg.shape[1]
    tm = _row_tile(m, tm_target)
    row = lambda w: pl.BlockSpec((tm, w), lambda i: (i, 0))
    args, specs = [x], [row(d)]
    if merge is not None:
        oa, ob, sma, smb, wpa, wpb, wo = merge
        args += [oa, ob, sma, smb, wpa, wpb, wo]
        specs += [row(oa.shape[1]), row(ob.shape[1]), row(d), row(d),
                  _const_spec(wpa.shape), _const_spec(wpb.shape), _const_spec(wo.shape)]
    args += [g.reshape(1, d), wg, wu, wd]
    specs += [_const_spec((1, d)), _const_spec(wg.shape), _const_spec(wu.shape), _const_spec(wd.shape)]
    if final_g is not None:
        args.append(final_g.reshape(1, d))
        specs.append(_const_spec((1, d)))
    kern = functools.partial(_ffn_kernel, f_chunks=_chunks(f, 1024),
                             has_merge=merge is not None, has_final=final_g is not None)
    return pl.pallas_call(
        kern, grid=(m // tm,), in_specs=specs, out_specs=row(d),
        out_shape=jax.ShapeDtypeStruct((m, d), F32),
        compiler_params=_params(("parallel",)), name="ffn",
    )(*args)


def _win_kernel(x_ref, g_ref, wa_ref, wb_ref, wgd_ref, wgu_ref, bg_ref,
                qa_ref, ka_ref, kab_ref, va_ref, vab_ref, qb_ref, kb_ref, vb_ref,
                la_ref, sr_ref, sma_ref, smb_ref, *, aw, bkw, bvw, d, heads, native_kv, qk_scale_a, qk_scale_b):
    h = _rms(x_ref[...], g_ref[...]).astype(BF16)
    p = _dot(h, wa_ref[:, :3 * aw])
    qa_ref[...] = p[:, :aw] * qk_scale_a
    ka = p[:, aw:2 * aw]
    va = p[:, 2 * aw:3 * aw]
    if native_kv:
        ka_ref[...] = ka.T
        hv = aw // heads
        for hd in range(heads):
            va_ref[pl.ds(hd, ka.shape[0], stride=heads), :] = va[:, hd * hv:(hd + 1) * hv]
    else:
        ka_ref[...] = ka
        va_ref[...] = va
    kab_ref[...] = ka.astype(BF16)
    vab_ref[...] = va.astype(BF16)
    p = _dot(h, wa_ref[:, 3 * aw:])
    qb_ref[...] = p[:, :bkw] * qk_scale_b
    kb_ref[...] = p[:, bkw:2 * bkw]
    vb_ref[...] = p[:, 2 * bkw:].astype(vb_ref.dtype)
    p = _dot(h, wb_ref[:, :bvw])
    sr_ref[...] = (p * _sigmoid(p)).astype(sr_ref.dtype)
    p = _dot(h, wb_ref[:, bvw:bvw + d])
    sma_ref[...] = _sigmoid(p).astype(sma_ref.dtype)
    p = _dot(h, wb_ref[:, bvw + d:])
    smb_ref[...] = _sigmoid(p).astype(smb_ref.dtype)
    gdown = _dot(h, wgd_ref[...])
    g_hi = gdown.astype(BF16)
    g_lo = (gdown - g_hi.astype(F32)).astype(BF16)
    zz = _dot(g_hi, wgu_ref[...])
    z = zz[:, :bkw] + zz[:, bkw:] + _dot(g_lo, wgu_ref[:, :bkw]) + bg_ref[...]
    la_ref[...] = (jnp.minimum(z, 0.0) - jnp.log(1.0 + jnp.exp(-jnp.abs(z)))) * (1.0 / GATE_TAU)


def _win(x, g, w_parts, wgu_pad, bg, dims, narrow, seq=None, tm_target=512):
    m, d = x.shape
    aw, bkw, bvw, qk_a, qk_b, heads = dims
    native_kv = seq is not None
    tm = _row_tile(seq if native_kv else m, tm_target)
    nd = BF16 if narrow else F32
    row = lambda w: pl.BlockSpec((tm, w), lambda i: (i, 0))
    outs = [(aw, F32), (aw, F32), (aw, BF16), (aw, F32), (aw, BF16), (bkw, F32), (bkw, F32), (bvw, nd),
            (bkw, F32), (bvw, nd), (d, BF16), (d, BF16)]
    out_specs = [row(w) for w, _ in outs]
    out_shape = [jax.ShapeDtypeStruct((m, w), t) for w, t in outs]
    if native_kv:
        nt = seq // tm
        out_specs[1] = pl.BlockSpec((aw, tm), lambda i: (i // nt, i % nt))
        out_shape[1] = jax.ShapeDtypeStruct((m // seq * aw, seq), F32)
        out_specs[3] = pl.BlockSpec((tm * heads, aw // heads), lambda i: (i, 0))
        out_shape[3] = jax.ShapeDtypeStruct((m * heads, aw // heads), F32)
    kern = functools.partial(_win_kernel, aw=aw, bkw=bkw, bvw=bvw, d=d, heads=heads, native_kv=native_kv,
                             qk_scale_a=qk_a ** -0.5, qk_scale_b=qk_b ** -0.5)
    return pl.pallas_call(
        kern, grid=(m // tm,),
        in_specs=[row(d), _const_spec((1, d))] + [_const_spec(w.shape) for w in w_parts]
                 + [_const_spec(wgu_pad.shape), _const_spec((1, bkw))],
        out_specs=out_specs, out_shape=out_shape,
        compiler_params=_params(("parallel",)), name="w_in",
    )(x, g.reshape(1, d), *w_parts, wgu_pad, bg.reshape(1, bkw))


def _lam(lq_ref, lk_ref, lam_init):
    lq = lq_ref[...]
    lk = lk_ref[...]
    s = jnp.sum(lq * lk, axis=-1, keepdims=True)
    e = jnp.exp(s)
    return e[0:1, :] - e[1:2, :] + lam_init


def _attn_prompt_kernel(q_ref, k_ref, v_ref, lq_ref, lk_ref, g_ref, o_ref, acc_ref, ml_ref, sa_ref, sb_ref,
                        *, t, kr, qk, nh, lam_init):
    i = pl.program_id(2)
    hw = 2 * qk
    qcat = []
    for h in range(nh):
        q = q_ref[:, h * hw:(h + 1) * hw] * math.log2(math.e)
        lane = lax.broadcasted_iota(jnp.int32, q.shape, 1)
        qcat.append(jnp.concatenate([jnp.where(lane < qk, q, 0.0), jnp.where(lane >= qk, q, 0.0)],
                                    axis=0).astype(BF16))
    acc_ref[...] = jnp.zeros_like(acc_ref)

    tk = kr * t

    def scores_to(s_ref, j):
        off = pl.multiple_of(j * tk, tk)
        for h in range(nh):
            s_ref[h] = _dot_nt(k_ref[pl.ds(off, tk), h * hw:(h + 1) * hw], qcat[h])

    def consume(j, s_ref, masked):
        off = pl.multiple_of(j * tk, tk)
        ps, alphas = [], []
        for h in range(nh):
            m = ml_ref[h, 0:1, :]
            l = ml_ref[h, 1:2, :]
            s = s_ref[h]
            if masked:
                d = (lax.broadcasted_iota(jnp.int32, s.shape, 0)
                     - lax.broadcasted_iota(jnp.int32, s.shape, 1) % t)
                s = jnp.where(d <= i * t - off, s, NEG)
            m_new = jnp.maximum(m, jnp.max(s, axis=0, keepdims=True))
            alpha = jnp.exp2(m - m_new)
            p = jnp.exp2(s - m_new)
            ml_ref[h, 0:1, :] = m_new
            ml_ref[h, 1:2, :] = alpha * l + jnp.sum(p, axis=0, keepdims=True)
            ps.append(p.astype(BF16))
            alphas.append(alpha)
        pvs = [_dot_tn(v_ref[pl.ds(off, tk), h * hw:(h + 1) * hw], ps[h]) for h in range(nh)]
        for h in range(nh):
            acc_ref[h] = alphas[h] * acc_ref[h] + pvs[h]

    nfull = i // kr
    for h in range(nh):
        ml_ref[h, 0:1, :] = jnp.full((1, 2 * t), NEG, F32)
        ml_ref[h, 1:2, :] = jnp.zeros((1, 2 * t), F32)
    scores_to(sa_ref, 0)

    def body(jj, carry):
        j = 2 * jj
        scores_to(sb_ref, j + 1)
        consume(j, sa_ref, False)
        scores_to(sa_ref, j + 2)
        consume(j + 1, sb_ref, False)
        return carry

    npairs = nfull // 2
    lax.fori_loop(0, npairs, body, 0)

    @pl.when(nfull % 2 == 1)
    def _():
        scores_to(sb_ref, nfull)
        consume(nfull - 1, sa_ref, False)
        consume(nfull, sb_ref, True)

    @pl.when(nfull % 2 == 0)
    def _():
        consume(nfull, sa_ref, True)

    lam = _lam(lq_ref, lk_ref, lam_init)
    for h in range(nh):
        a = acc_ref[h] / ml_ref[h, 1:2, :]
        o = (a[:, :t] - lam * a[:, t:]).T
        o_ref[:, h * hw:(h + 1) * hw] = _rms(o, g_ref[...]).astype(o_ref.dtype)


def _attn_prompt(qa, kab, vab, lq, lk, g_eff, batch, seq, heads, qk, lam_init, t_target=256, kr_target=2,
                 nh_target=4):
    mp, aw = qa.shape
    hw = aw // heads
    t = _row_tile(seq, t_target)
    nq = seq // t
    kr = max(r for r in range(1, kr_target + 1) if nq % r == 0)
    nh = max(n for n in range(1, nh_target + 1) if heads % n == 0)
    kern = functools.partial(_attn_prompt_kernel, t=t, kr=kr, qk=qk, nh=nh, lam_init=lam_init)
    kv_spec = pl.BlockSpec((seq, nh * hw), lambda b, h, i: (b, h), pipeline_mode=pl.Buffered(1))
    return pl.pallas_call(
        kern, grid=(batch, heads // nh, nq),
        in_specs=[pl.BlockSpec((t, nh * hw), lambda b, h, i: (b * nq + i, h)),
                  kv_spec, kv_spec,
                  pl.BlockSpec((2, qk), lambda b, h, i: (0, 0)),
                  pl.BlockSpec((2, qk), lambda b, h, i: (0, 0)),
                  pl.BlockSpec((1, hw), lambda b, h, i: (0, 0))],
        out_specs=pl.BlockSpec((t, nh * hw), lambda b, h, i: (b * nq + i, h)),
        out_shape=jax.ShapeDtypeStruct((mp, aw), BF16),
        scratch_shapes=[pltpu.VMEM((nh, hw, 2 * t), F32), pltpu.VMEM((nh, 8, 2 * t), F32),
                        pltpu.VMEM((nh, kr * t, 2 * t), F32), pltpu.VMEM((nh, kr * t, 2 * t), F32)],
        compiler_params=_params(("parallel", "parallel", "arbitrary")), name="attn_prompt",
    )(qa, kab, vab, lq, lk, g_eff.reshape(1, hw))


def _attn_sample_kernel(pt_ref, q_ref, kn_ref, vn_ref, lq_ref, lk_ref, g_ref, *rest,
                        pp, l, heads, qk, page, lam_init):
    kc_refs = rest[:pp]
    vc_refs = rest[pp:2 * pp]
    o_ref = rest[2 * pp]
    qbd_ref, m_ref, l_ref, acc_ref = rest[2 * pp + 1:]
    j = pl.program_id(1)
    hm = 2 * heads
    hw = 2 * qk

    def online(s, v_heads, m, lsum, acc):
        m_new = jnp.maximum(m, jnp.max(s, axis=-1, keepdims=True))
        alpha = jnp.exp(m - m_new)
        p = jnp.exp(s - m_new)
        l_new = alpha * lsum + jnp.sum(p, axis=-1, keepdims=True)
        pv = [_dot(p[2 * h * l:(2 * h + 2) * l, :].astype(BF16), v_heads[h]) for h in range(heads)]
        return m_new, l_new, alpha * acc + jnp.concatenate(pv, axis=0)

    @pl.when(j == 0)
    def _():
        qt = jnp.concatenate([q_ref[...]] * hm, axis=0)
        rg = lax.broadcasted_iota(jnp.int32, qt.shape, 0) // l
        lg = lax.broadcasted_iota(jnp.int32, qt.shape, 1) // qk
        qbd = jnp.where(rg == lg, qt, 0.0).astype(BF16)
        qbd_ref[...] = qbd
        pad = jnp.zeros((page - l, heads * hw), F32)
        kn = jnp.concatenate([kn_ref[...], pad], axis=0).astype(BF16)
        vn = jnp.concatenate([vn_ref[...], pad], axis=0).astype(BF16)
        s = _dot_nt(qbd, kn)
        qi = lax.broadcasted_iota(jnp.int32, s.shape, 0) % l
        kj = lax.broadcasted_iota(jnp.int32, s.shape, 1)
        s = jnp.where(kj <= qi, s, NEG)
        m0 = jnp.full((hm * l, 1), NEG, F32)
        z0 = jnp.zeros((hm * l, 1), F32)
        m, lsum, acc = online(s, [vn[:, h * hw:(h + 1) * hw] for h in range(heads)],
                              m0, z0, jnp.zeros((hm * l, hw), F32))
        m_ref[...] = m
        l_ref[...] = lsum
        acc_ref[...] = acc

    qbd = qbd_ref[...]
    s_list = [_dot(qbd, kc[...].astype(BF16)) for kc in kc_refs]
    s = jnp.concatenate(s_list, axis=1) if pp > 1 else s_list[0]
    v_heads = []
    for h in range(heads):
        vh = [vc[pl.ds(h, page, stride=heads), :] for vc in vc_refs]
        v_heads.append((jnp.concatenate(vh, axis=0) if pp > 1 else vh[0]).astype(BF16))
    m, lsum, acc = online(s, v_heads, m_ref[...], l_ref[...], acc_ref[...])
    m_ref[...] = m
    l_ref[...] = lsum
    acc_ref[...] = acc

    @pl.when(j == pl.num_programs(1) - 1)
    def _():
        lam = _lam(lq_ref, lk_ref, lam_init)
        a = acc_ref[...] / l_ref[...]
        outs = [_rms(a[2 * h * l:(2 * h + 1) * l, :] - lam * a[(2 * h + 1) * l:(2 * h + 2) * l, :], g_ref[...])
                for h in range(heads)]
        o_ref[...] = jnp.concatenate(outs, axis=1).astype(o_ref.dtype)


def _attn_sample(qa, ka, va, lq, lk, g_eff, cache_kt, cache_vr, page_table, layer, dec_batch, l, heads, qk,
                 lam_init, pp_target=16):
    ms, aw = qa.shape
    hw = aw // heads
    page = cache_kt.shape[3]
    n_pages = page_table.shape[1]
    pp = max(p for p in range(1, pp_target + 1) if n_pages % p == 0)
    assert l % 8 == 0 and l <= page
    hm = 2 * heads
    kern = functools.partial(_attn_sample_kernel, pp=pp, l=l, heads=heads, qk=qk, page=page, lam_init=lam_init)
    rows = pl.BlockSpec((l, aw), lambda b, j, pt: (b, 0))
    small = lambda shape: pl.BlockSpec(shape, lambda b, j, pt: (0, 0))

    def page_spec(n, shape):
        return pl.BlockSpec((None, None) + shape,
                            lambda b, j, pt: (layer, pt[b * n_pages + j * pp + n], 0, 0))

    grid_spec = pltpu.PrefetchScalarGridSpec(
        num_scalar_prefetch=1, grid=(dec_batch, n_pages // pp),
        in_specs=[rows, rows, rows, small((2, qk)), small((2, qk)), small((1, hw))]
                 + [page_spec(n, (aw, page)) for n in range(pp)]
                 + [page_spec(n, (page * heads, hw)) for n in range(pp)],
        out_specs=rows,
        scratch_shapes=[pltpu.VMEM((hm * l, aw), BF16), pltpu.VMEM((hm * l, 1), F32),
                        pltpu.VMEM((hm * l, 1), F32), pltpu.VMEM((hm * l, hw), F32)])
    return pl.pallas_call(
        kern, grid_spec=grid_spec, out_shape=jax.ShapeDtypeStruct((ms, aw), F32),
        compiler_params=_params(("parallel", "arbitrary")), name="attn_sample",
    )(page_table.reshape(-1), qa, ka, va, lq, lk, g_eff.reshape(1, hw),
      *([cache_kt] * pp), *([cache_vr] * pp))


def _gla_kernel(q_ref, k_ref, v_ref, la_ref, sr_ref, bn_ref, s0_ref, o_ref, sf_ref, st_ref,
                *, rows, chunk, sub, heads, dk, dv):
    j = pl.program_id(1)
    kw = heads * dk
    vw = heads * dv

    @pl.when(j == 0)
    def _():
        for h in range(heads):
            st_ref[:, h * dk:(h + 1) * dk] = s0_ref[h].T

    tri = (lax.broadcasted_iota(jnp.int32, (chunk, chunk), 0)
           >= lax.broadcasted_iota(jnp.int32, (chunk, chunk), 1)).astype(BF16)
    head_of = lambda n: lax.broadcasted_iota(jnp.int32, (n, kw), 1) // dk

    def stack_heads(x):
        lh = head_of(x.shape[0])
        return jnp.concatenate([jnp.where(lh == h, x, 0.0) for h in range(heads)], axis=0).astype(BF16)

    starts = list(range(0, rows, chunk))
    subs = list(range(0, chunk, sub))
    qcs = [q_ref[c0:c0 + chunk, :] for c0 in starts]
    kcs = [k_ref[c0:c0 + chunk, :] for c0 in starts]
    vcs = [v_ref[c0:c0 + chunk, :].astype(BF16) for c0 in starts]

    def decay_sum(la):
        hi = la.astype(BF16)
        r1 = la - hi.astype(F32)
        mid = r1.astype(BF16)
        lo = (r1 - mid.astype(F32)).astype(BF16)
        parts = _dot(tri, jnp.concatenate([hi, mid, lo], axis=1))
        return parts[:, :kw] + parts[:, kw:2 * kw] + parts[:, 2 * kw:]

    bs = [decay_sum(la_ref[c0:c0 + chunk, :]) for c0 in starts]
    qsts, kds, decs, qss, kis = [], [], [], [], []
    for qc, kc, b in zip(qcs, kcs, bs):
        b_last = b[chunk - 1:chunk, :]
        qsts.append(stack_heads(qc * jnp.exp(b)))
        kds.append(stack_heads(kc * jnp.exp(b_last - b)))
        decs.append(jnp.exp(b_last))
        for lo in subs:
            hi = lo + sub
            ref = b[lo - 1:lo, :] if lo > 0 else jnp.zeros((1, kw), F32)
            qss.append(stack_heads(qc[lo:hi, :] * jnp.exp(b[lo:hi, :] - ref)))
            kis.append((kc[:hi, :] * jnp.exp(ref - b[:hi, :])).astype(BF16))
    vsts = [jnp.concatenate([vc[:, h * dv:(h + 1) * dv] for h in range(heads)], axis=0) for vc in vcs]
    upds = [_dot_tn(vst, kd) for vst, kd in zip(vsts, kds)]
    atts = [_dot_nt(qs, ki) for qs, ki in zip(qss, kis)]
    masked = []
    for n, att in enumerate(atts):
        lo = subs[n % len(subs)]
        tt = lax.broadcasted_iota(jnp.int32, att.shape, 0) % sub + lo
        ss = lax.broadcasted_iota(jnp.int32, att.shape, 1)
        masked.append(jnp.where(ss <= tt, att, 0.0).astype(BF16))
    ovs = [_dot(att, vcs[n // len(subs)][:subs[n % len(subs)] + sub, :]) for n, att in enumerate(masked)]
    pre = []
    for c in range(len(starts)):
        parts = [jnp.concatenate([ov[h * sub:(h + 1) * sub, h * dv:(h + 1) * dv] for h in range(heads)], axis=1)
                 for ov in ovs[c * len(subs):(c + 1) * len(subs)]]
        intra = jnp.concatenate(parts, axis=0) if len(parts) > 1 else parts[0]
        pre.append((qsts[c], upds[c], decs[c], intra))

    st = st_ref[...]
    for n, c0 in enumerate(range(0, rows, chunk)):
        qst, upd, dec, intra = pre[n]
        r = _dot_nt(qst, st.astype(BF16))
        o = jnp.concatenate([r[h * chunk:(h + 1) * chunk, :] for h in range(heads)], axis=1) + intra
        st = st * dec + upd
        sr = sr_ref[c0:c0 + chunk, :].astype(F32)
        outs = [_rms(o[:, h * dv:(h + 1) * dv], bn_ref[...]) * sr[:, h * dv:(h + 1) * dv]
                for h in range(heads)]
        o_ref[c0:c0 + chunk, :] = jnp.concatenate(outs, axis=1).astype(o_ref.dtype)
    st_ref[...] = st

    @pl.when(j == pl.num_programs(1) - 1)
    def _():
        for h in range(heads):
            sf_ref[h] = st[:, h * dk:(h + 1) * dk].T


def _gla(qb, kb, vb, la, sr, bn, s0, batch, seq, out_dtype, chunk_target=64, rows_target=512):
    m, kw = qb.shape
    vw = vb.shape[1]
    _, heads, dk, dv = s0.shape
    chunk = min(chunk_target, seq)
    assert seq % chunk == 0 and chunk % 8 == 0
    sub = min(16, chunk)
    assert chunk % sub == 0
    rows = chunk * max(1, min(rows_target // chunk, seq // chunk))
    while seq % rows:
        rows -= chunk
    nj = seq // rows
    kern = functools.partial(_gla_kernel, rows=rows, chunk=chunk, sub=sub, heads=heads, dk=dk, dv=dv)
    row = lambda w: pl.BlockSpec((rows, w), lambda b, j: (b * nj + j, 0))
    state = pl.BlockSpec((None, heads, dk, dv), lambda b, j: (b, 0, 0, 0))
    return pl.pallas_call(
        kern, grid=(batch, nj),
        in_specs=[row(kw), row(kw), row(vw), row(kw), row(vw),
                  pl.BlockSpec((1, dv), lambda b, j: (0, 0)), state],
        out_specs=[row(vw), state],
        out_shape=[jax.ShapeDtypeStruct((m, vw), out_dtype), jax.ShapeDtypeStruct(s0.shape, F32)],
        scratch_shapes=[pltpu.VMEM((dv, kw), F32)],
        compiler_params=_params(("parallel", "arbitrary")), name="gla",
    )(qb, kb, vb, la, sr, bn.reshape(1, dv), s0)


def kernel(x_prompt, x_sample, cache_k, cache_v, state_gla, page_table,
           ffn1_norm, ffn1_wg, ffn1_wu, ffn1_wd, mix_norm, w_in, w_gate_up, b_gate,
           lam_q, lam_k, a_subln, b_norm, w_pa, w_pb, w_o,
           ffn2_norm, ffn2_wg, ffn2_wu, ffn2_wd, final_norm):
    batch, seq, d = x_prompt.shape
    dec_batch, dec_seq, _ = x_sample.shape
    depth, n_phys, page, a_heads, _, a_qk = cache_k.shape
    a_v = cache_v.shape[-1]
    _, _, b_heads, b_dk, b_dv = state_gla.shape
    rank = w_gate_up.shape[1]
    aw, bkw, bvw = a_heads * a_v, b_heads * b_dk, b_heads * b_dv
    assert a_v == 2 * a_qk and aw == a_heads * 2 * a_qk

    o_gd = 3 * aw + 2 * bkw + bvw
    w_a = w_in[:, :, :o_gd].astype(BF16)
    w_b = w_in[:, :, o_gd + rank:].astype(BF16)
    w_gd = jnp.pad(w_in[:, :, o_gd:o_gd + rank], ((0, 0), (0, 0), (0, LANES - rank))).astype(BF16)
    wgu_f = jnp.concatenate([w_gate_up, jnp.zeros((depth, LANES - rank, bkw), F32)], axis=1)
    wgu_hi = wgu_f.astype(BF16)
    wgu_pad = jnp.concatenate([wgu_hi, (wgu_f - wgu_hi.astype(F32)).astype(BF16)], axis=-1)
    bf = lambda w: w.astype(BF16)
    f1g, f1u, f1d = bf(ffn1_wg), bf(ffn1_wu), bf(ffn1_wd)
    f2g, f2u, f2d = bf(ffn2_wg), bf(ffn2_wu), bf(ffn2_wd)
    wpa, wpb, wo = bf(w_pa), bf(w_pb), bf(w_o)
    ck = jnp.transpose(cache_k, (0, 1, 3, 4, 5, 2)).reshape(depth, n_phys, aw, page)
    cv = cache_v.reshape(depth, n_phys, page * a_heads, a_v)

    xp = x_prompt.reshape(batch * seq, d)
    xs = x_sample.reshape(dec_batch * dec_seq, d)
    s0_prompt = jnp.zeros((batch, b_heads, b_dk, b_dv), F32)
    dims = (aw, bkw, bvw, a_qk, b_dk, a_heads)
    kp, vp, sp, ks, vs, ss = [], [], [], [], [], []
    for l in range(depth):
        lam_init = 0.8 - 0.6 * math.exp(-0.3 * l)
        g_eff = a_subln[l] * (1.0 - lam_init)
        last = l == depth - 1

        xp = _ffn(xp, ffn1_norm[l], f1g[l], f1u[l], f1d[l])
        xs = _ffn(xs, ffn1_norm[l], f1g[l], f1u[l], f1d[l])

        qa, ka, kab, va, vab, qb, kb, vb, la, sr, sma, smb = _win(
            xp, mix_norm[l], (w_a[l], w_b[l], w_gd[l]), wgu_pad[l], b_gate[l], dims, narrow=True, seq=seq)
        oa = _attn_prompt(qa, kab, vab, lam_q[l], lam_k[l], g_eff, batch, seq, a_heads, a_qk, lam_init)
        ob, sfin = _gla(qb, kb, vb, la, sr, b_norm[l], s0_prompt, batch, seq, BF16)
        xp = _ffn(xp, ffn2_norm[l], f2g[l], f2u[l], f2d[l],
                  merge=(oa, ob, sma, smb, wpa[l], wpb[l], wo[l]), final_g=final_norm if last else None)
        kp.append(ka); vp.append(va); sp.append(sfin)

        qa, ka, kab, va, vab, qb, kb, vb, la, sr, sma, smb = _win(
            xs, mix_norm[l], (w_a[l], w_b[l], w_gd[l]), wgu_pad[l], b_gate[l], dims, narrow=False)
        oa = _attn_sample(qa, ka, va, lam_q[l], lam_k[l], g_eff, ck, cv, page_table, l,
                          dec_batch, dec_seq, a_heads, a_qk, lam_init)
        ob, sfin = _gla(qb, kb, vb, la, sr, b_norm[l], state_gla[l], dec_batch, dec_seq, F32)
        xs = _ffn(xs, ffn2_norm[l], f2g[l], f2u[l], f2d[l],
                  merge=(oa, ob, sma, smb, wpa[l], wpb[l], wo[l]), final_g=final_norm if last else None)
        ks.append(ka); vs.append(va); ss.append(sfin)

    y_prompt = xp.reshape(batch, seq, d)
    y_sample = xs.reshape(dec_batch, dec_seq, d)
    k_prompt = jnp.transpose(jnp.stack(kp).reshape(depth, batch, a_heads, 2, a_qk, seq), (0, 1, 5, 2, 3, 4))
    v_prompt = jnp.stack(vp).reshape(depth, batch, seq, a_heads, a_v)
    gla_prompt = jnp.stack(sp)
    k_sample = jnp.stack(ks).reshape(depth, dec_batch, dec_seq, a_heads, 2, a_qk)
    v_sample = jnp.stack(vs).reshape(depth, dec_batch, dec_seq, a_heads, a_v)
    gla_sample = jnp.stack(ss)
    return (y_prompt, y_sample, k_prompt, v_prompt, gla_prompt, k_sample, v_sample, gla_sample)
```

```python
import functools
import math

import jax
import jax.numpy as jnp
from jax import lax
from jax.experimental import pallas as pl
from jax.experimental.pallas import tpu as pltpu

EPS = 1e-6
GATE_TAU = 16.0
F32 = jnp.float32
BF16 = jnp.bfloat16
NEG = -1e30

LANES = 128
VMEM_LIMIT = 56 * 1024 * 1024


def _dot(a, b):
    return jnp.dot(a, b, preferred_element_type=F32)


def _dot_nt(a, b):
    return lax.dot_general(a, b, (((1,), (1,)), ((), ())), preferred_element_type=F32)


def _dot_tn(a, b):
    return lax.dot_general(a, b, (((0,), (0,)), ((), ())), preferred_element_type=F32)


def _rms(x, g):
    return x * lax.rsqrt(jnp.mean(x * x, axis=-1, keepdims=True) + EPS) * g


def _sigmoid(x):
    return 1.0 / (1.0 + jnp.exp(-x))


def _row_tile(m, target):
    best = None
    for t in range(8, min(m, target) + 1, 8):
        if m % t == 0:
            best = t
    assert best is not None, m
    return best


def _chunks(n, size):
    out, s = [], 0
    while s < n:
        out.append((s, min(size, n - s)))
        s += size
    return out


def _const_spec(shape):
    nd = len(shape)
    return pl.BlockSpec(shape, lambda *_: (0,) * nd, pipeline_mode=pl.Buffered(1))


def _params(sem):
    return pltpu.CompilerParams(dimension_semantics=sem, vmem_limit_bytes=VMEM_LIMIT)


def _ffn_kernel(*refs, f_chunks, has_merge, has_final):
    it = iter(refs)
    x_ref = next(it)
    if has_merge:
        oa_ref, ob_ref, sma_ref, smb_ref, wpa_ref, wpb_ref, wo_ref = (next(it) for _ in range(7))
    g_ref, wg_ref, wu_ref, wd_ref = (next(it) for _ in range(4))
    if has_final:
        gf_ref = next(it)
    o_ref = next(it)

    x = x_ref[...]
    if has_merge:
        ya = _dot(oa_ref[...].astype(BF16), wpa_ref[...])
        yb = _dot(ob_ref[...].astype(BF16), wpb_ref[...])
        y = sma_ref[...].astype(F32) * ya + smb_ref[...].astype(F32) * yb
        x = x + _dot(y.astype(BF16), wo_ref[...])
    h = _rms(x, g_ref[...]).astype(BF16)
    acc = None
    for s, n in f_chunks:
        gate = _dot(h, wg_ref[:, s:s + n])
        up = _dot(h, wu_ref[:, s:s + n])
        act = (gate * _sigmoid(gate) * up).astype(BF16)
        y = _dot(act, wd_ref[s:s + n, :])
        acc = y if acc is None else acc + y
    out = x + 0.5 * acc
    if has_final:
        out = _rms(out, gf_ref[...])
    o_ref[...] = out


def _ffn(x, g, wg, wu, wd, merge=None, final_g=None, tm_target=512):
    m, d = x.shape
    f = wg.shape[1]
    tm = _row_tile(m, tm_target)
    row = lambda w: pl.BlockSpec((tm, w), lambda i: (i, 0))
    args, specs = [x], [row(d)]
    if merge is not None:
        oa, ob, sma, smb, wpa, wpb, wo = merge
        args += [oa, ob, sma, smb, wpa, wpb, wo]
        specs += [row(oa.shape[1]), row(ob.shape[1]), row(d), row(d),
                  _const_spec(wpa.shape), _const_spec(wpb.shape), _const_spec(wo.shape)]
    args += [g.reshape(1, d), wg, wu, wd]
    specs += [_const_spec((1, d)), _const_spec(wg.shape), _const_spec(wu.shape), _const_spec(wd.shape)]
    if final_g is not None:
        args.append(final_g.reshape(1, d))
        specs.append(_const_spec((1, d)))
    kern = functools.partial(_ffn_kernel, f_chunks=_chunks(f, 1024),
                             has_merge=merge is not None, has_final=final_g is not None)
    return pl.pallas_call(
        kern, grid=(m // tm,), in_specs=specs, out_specs=row(d),
        out_shape=jax.ShapeDtypeStruct((m, d), F32),
        compiler_params=_params(("parallel",)), name="ffn",
    )(*args)


def _win_kernel(x_ref, g_ref, wa_ref, wb_ref, wgd_ref, wgu_ref, bg_ref,
                qa_ref, ka_ref, kab_ref, va_ref, vab_ref, qb_ref, kb_ref, vb_ref,
                la_ref, sr_ref, sma_ref, smb_ref, *, aw, bkw, bvw, d, heads, native_kv, qk_scale_a, qk_scale_b):
    h = _rms(x_ref[...], g_ref[...]).astype(BF16)
    p = _dot(h, wa_ref[:, :3 * aw])
    qa_ref[...] = p[:, :aw] * qk_scale_a
    ka = p[:, aw:2 * aw]
    va = p[:, 2 * aw:3 * aw]
    if native_kv:
        ka_ref[...] = ka.T
        hv = aw // heads
        for hd in range(heads):
            va_ref[pl.ds(hd, ka.shape[0], stride=heads), :] = va[:, hd * hv:(hd + 1) * hv]
    else:
        ka_ref[...] = ka
        va_ref[...] = va
    kab_ref[...] = ka.astype(BF16)
    vab_ref[...] = va.astype(BF16)
    p = _dot(h, wa_ref[:, 3 * aw:])
    qb_ref[...] = p[:, :bkw] * qk_scale_b
    kb_ref[...] = p[:, bkw:2 * bkw]
    vb_ref[...] = p[:, 2 * bkw:].astype(vb_ref.dtype)
    p = _dot(h, wb_ref[:, :bvw])
    sr_ref[...] = (p * _sigmoid(p)).astype(sr_ref.dtype)
    p = _dot(h, wb_ref[:, bvw:bvw + d])
    sma_ref[...] = _sigmoid(p).astype(sma_ref.dtype)
    p = _dot(h, wb_ref[:, bvw + d:])
    smb_ref[...] = _sigmoid(p).astype(smb_ref.dtype)
    gdown = _dot(h, wgd_ref[...])
    g_hi = gdown.astype(BF16)
    g_lo = (gdown - g_hi.astype(F32)).astype(BF16)
    zz = _dot(g_hi, wgu_ref[...])
    z = zz[:, :bkw] + zz[:, bkw:] + _dot(g_lo, wgu_ref[:, :bkw]) + bg_ref[...]
    la_ref[...] = (jnp.minimum(z, 0.0) - jnp.log(1.0 + jnp.exp(-jnp.abs(z)))) * (1.0 / GATE_TAU)


def _win(x, g, w_parts, wgu_pad, bg, dims, narrow, seq=None, tm_target=512):
    m, d = x.shape
    aw, bkw, bvw, qk_a, qk_b, heads = dims
    native_kv = seq is not None
    tm = _row_tile(seq if native_kv else m, tm_target)
    nd = BF16 if narrow else F32
    row = lambda w: pl.BlockSpec((tm, w), lambda i: (i, 0))
    outs = [(aw, F32), (aw, F32), (aw, BF16), (aw, F32), (aw, BF16), (bkw, F32), (bkw, F32), (bvw, nd),
            (bkw, F32), (bvw, nd), (d, BF16), (d, BF16)]
    out_specs = [row(w) for w, _ in outs]
    out_shape = [jax.ShapeDtypeStruct((m, w), t) for w, t in outs]
    if native_kv:
        nt = seq // tm
        out_specs[1] = pl.BlockSpec((aw, tm), lambda i: (i // nt, i % nt))
        out_shape[1] = jax.ShapeDtypeStruct((m // seq * aw, seq), F32)
        out_specs[3] = pl.BlockSpec((tm * heads, aw // heads), lambda i: (i, 0))
        out_shape[3] = jax.ShapeDtypeStruct((m * heads, aw // heads), F32)
    kern = functools.partial(_win_kernel, aw=aw, bkw=bkw, bvw=bvw, d=d, heads=heads, native_kv=native_kv,
                             qk_scale_a=qk_a ** -0.5, qk_scale_b=qk_b ** -0.5)
    return pl.pallas_call(
        kern, grid=(m // tm,),
        in_specs=[row(d), _const_spec((1, d))] + [_const_spec(w.shape) for w in w_parts]
                 + [_const_spec(wgu_pad.shape), _const_spec((1, bkw))],
        out_specs=out_specs, out_shape=out_shape,
        compiler_params=_params(("parallel",)), name="w_in",
    )(x, g.reshape(1, d), *w_parts, wgu_pad, bg.reshape(1, bkw))


def _lam(lq_ref, lk_ref, lam_init):
    lq = lq_ref[...]
    lk = lk_ref[...]
    s = jnp.sum(lq * lk, axis=-1, keepdims=True)
    e = jnp.exp(s)
    return e[0:1, :] - e[1:2, :] + lam_init


def _attn_prompt_kernel(q_ref, k_ref, v_ref, lq_ref, lk_ref, g_ref, o_ref, acc_ref, ml_ref, sa_ref, sb_ref,
                        *, t, kr, qk, nh, lam_init):
    i = pl.program_id(2)
    hw = 2 * qk
    qcat = []
    for h in range(nh):
        q = q_ref[:, h * hw:(h + 1) * hw] * math.log2(math.e)
        lane = lax.broadcasted_iota(jnp.int32, q.shape, 1)
        qcat.append(jnp.concatenate([jnp.where(lane < qk, q, 0.0), jnp.where(lane >= qk, q, 0.0)],
                                    axis=0).astype(BF16))
    acc_ref[...] = jnp.zeros_like(acc_ref)

    tk = kr * t

    def scores_to(s_ref, j):
        off = pl.multiple_of(j * tk, tk)
        for h in range(nh):
            s_ref[h] = _dot_nt(k_ref[pl.ds(off, tk), h * hw:(h + 1) * hw], qcat[h])

    def consume(j, s_ref, masked):
        off = pl.multiple_of(j * tk, tk)
        ps, alphas = [], []
        for h in range(nh):
            m = ml_ref[h, 0:1, :]
            l = ml_ref[h, 1:2, :]
            s = s_ref[h]
            if masked:
                d = (lax.broadcasted_iota(jnp.int32, s.shape, 0)
                     - lax.broadcasted_iota(jnp.int32, s.shape, 1) % t)
                s = jnp.where(d <= i * t - off, s, NEG)
            m_new = jnp.maximum(m, jnp.max(s, axis=0, keepdims=True))
            alpha = jnp.exp2(m - m_new)
            p = jnp.exp2(s - m_new)
            ml_ref[h, 0:1, :] = m_new
            ml_ref[h, 1:2, :] = alpha * l + jnp.sum(p, axis=0, keepdims=True)
            ps.append(p.astype(BF16))
            alphas.append(alpha)
        pvs = [_dot_tn(v_ref[pl.ds(off, tk), h * hw:(h + 1) * hw], ps[h]) for h in range(nh)]
        for h in range(nh):
            acc_ref[h] = alphas[h] * acc_ref[h] + pvs[h]

    nfull = i // kr
    for h in range(nh):
        ml_ref[h, 0:1, :] = jnp.full((1, 2 * t), NEG, F32)
        ml_ref[h, 1:2, :] = jnp.zeros((1, 2 * t), F32)
    scores_to(sa_ref, 0)

    def body(jj, carry):
        j = 2 * jj
        scores_to(sb_ref, j + 1)
        consume(j, sa_ref, False)
        scores_to(sa_ref, j + 2)
        consume(j + 1, sb_ref, False)
        return carry

    npairs = nfull // 2
    lax.fori_loop(0, npairs, body, 0)

    @pl.when(nfull % 2 == 1)
    def _():
        scores_to(sb_ref, nfull)
        consume(nfull - 1, sa_ref, False)
        consume(nfull, sb_ref, True)

    @pl.when(nfull % 2 == 0)
    def _():
        consume(nfull, sa_ref, True)

    lam = _lam(lq_ref, lk_ref, lam_init)
    for h in range(nh):
        a = acc_ref[h] / ml_ref[h, 1:2, :]
        o = (a[:, :t] - lam * a[:, t:]).T
        o_ref[:, h * hw:(h + 1) * hw] = _rms(o, g_ref[...]).astype(o_ref.dtype)


def _attn_prompt(qa, kab, vab, lq, lk, g_eff, batch, seq, heads, qk, lam_init, t_target=256, kr_target=2,
                 nh_target=4):
    mp, aw = qa.shape
    hw = aw // heads
    t = _row_tile(seq, t_target)
    nq = seq // t
    kr = max(r for r in range(1, kr_target + 1) if nq % r == 0)
    nh = max(n for n in range(1, nh_target + 1) if heads % n == 0)
    kern = functools.partial(_attn_prompt_kernel, t=t, kr=kr, qk=qk, nh=nh, lam_init=lam_init)
    kv_spec = pl.BlockSpec((seq, nh * hw), lambda b, h, i: (b, h), pipeline_mode=pl.Buffered(1))
    return pl.pallas_call(
        kern, grid=(batch, heads // nh, nq),
        in_specs=[pl.BlockSpec((t, nh * hw), lambda b, h, i: (b * nq + i, h)),
                  kv_spec, kv_spec,
                  pl.BlockSpec((2, qk), lambda b, h, i: (0, 0)),
                  pl.BlockSpec((2, qk), lambda b, h, i: (0, 0)),
                  pl.BlockSpec((1, hw), lambda b, h, i: (0, 0))],
        out_specs=pl.BlockSpec((t, nh * hw), lambda b, h, i: (b * nq + i, h)),
        out_shape=jax.ShapeDtypeStruct((mp, aw), BF16),
        scratch_shapes=[pltpu.VMEM((nh, hw, 2 * t), F32), pltpu.VMEM((nh, 8, 2 * t), F32),
                        pltpu.VMEM((nh, kr * t, 2 * t), F32), pltpu.VMEM((nh, kr * t, 2 * t), F32)],
        compiler_params=_params(("parallel", "parallel", "arbitrary")), name="attn_prompt",
    )(qa, kab, vab, lq, lk, g_eff.reshape(1, hw))


def _attn_sample_kernel(pt_ref, q_ref, kn_ref, vn_ref, lq_ref, lk_ref, g_ref, *rest,
                        pp, l, heads, qk, page, lam_init):
    kc_refs = rest[:pp]
    vc_refs = rest[pp:2 * pp]
    o_ref = rest[2 * pp]
    qbd_ref, m_ref, l_ref, acc_ref = rest[2 * pp + 1:]
    j = pl.program_id(1)
    hm = 2 * heads
    hw = 2 * qk

    def online(s, v_heads, m, lsum, acc):
        m_new = jnp.maximum(m, jnp.max(s, axis=-1, keepdims=True))
        alpha = jnp.exp(m - m_new)
        p = jnp.exp(s - m_new)
        l_new = alpha * lsum + jnp.sum(p, axis=-1, keepdims=True)
        pv = [_dot(p[2 * h * l:(2 * h + 2) * l, :].astype(BF16), v_heads[h]) for h in range(heads)]
        return m_new, l_new, alpha * acc + jnp.concatenate(pv, axis=0)

    @pl.when(j == 0)
    def _():
        qt = jnp.concatenate([q_ref[...]] * hm, axis=0)
        rg = lax.broadcasted_iota(jnp.int32, qt.shape, 0) // l
        lg = lax.broadcasted_iota(jnp.int32, qt.shape, 1) // qk
        qbd = jnp.where(rg == lg, qt, 0.0).astype(BF16)
        qbd_ref[...] = qbd
        pad = jnp.zeros((page - l, heads * hw), F32)
        kn = jnp.concatenate([kn_ref[...], pad], axis=0).astype(BF16)
        vn = jnp.concatenate([vn_ref[...], pad], axis=0).astype(BF16)
        s = _dot_nt(qbd, kn)
        qi = lax.broadcasted_iota(jnp.int32, s.shape, 0) % l
        kj = lax.broadcasted_iota(jnp.int32, s.shape, 1)
        s = jnp.where(kj <= qi, s, NEG)
        m0 = jnp.full((hm * l, 1), NEG, F32)
        z0 = jnp.zeros((hm * l, 1), F32)
        m, lsum, acc = online(s, [vn[:, h * hw:(h + 1) * hw] for h in range(heads)],
                              m0, z0, jnp.zeros((hm * l, hw), F32))
        m_ref[...] = m
        l_ref[...] = lsum
        acc_ref[...] = acc

    qbd = qbd_ref[...]
    s_list = [_dot(qbd, kc[...].astype(BF16)) for kc in kc_refs]
    s = jnp.concatenate(s_list, axis=1) if pp > 1 else s_list[0]
    v_heads = []
    for h in range(heads):
        vh = [vc[pl.ds(h, page, stride=heads), :] for vc in vc_refs]
        v_heads.append((jnp.concatenate(vh, axis=0) if pp > 1 else vh[0]).astype(BF16))
    m, lsum, acc = online(s, v_heads, m_ref[...], l_ref[...], acc_ref[...])
    m_ref[...] = m
    l_ref[...] = lsum
    acc_ref[...] = acc

    @pl.when(j == pl.num_programs(1) - 1)
    def _():
        lam = _lam(lq_ref, lk_ref, lam_init)
        a = acc_ref[...] / l_ref[...]
        outs = [_rms(a[2 * h * l:(2 * h + 1) * l, :] - lam * a[(2 * h + 1) * l:(2 * h + 2) * l, :], g_ref[...])
                for h in range(heads)]
        o_ref[...] = jnp.concatenate(outs, axis=1).astype(o_ref.dtype)


def _attn_sample(qa, ka, va, lq, lk, g_eff, cache_kt, cache_vr, page_table, layer, dec_batch, l, heads, qk,
                 lam_init, pp_target=32):
    ms, aw = qa.shape
    hw = aw // heads
    page = cache_kt.shape[3]
    n_pages = page_table.shape[1]
    pp = max(p for p in range(1, pp_target + 1) if n_pages % p == 0)
    assert l % 8 == 0 and l <= page
    hm = 2 * heads
    kern = functools.partial(_attn_sample_kernel, pp=pp, l=l, heads=heads, qk=qk, page=page, lam_init=lam_init)
    rows = pl.BlockSpec((l, aw), lambda b, j, pt: (b, 0))
    small = lambda shape: pl.BlockSpec(shape, lambda b, j, pt: (0, 0))

    def page_spec(n, shape):
        return pl.BlockSpec((None, None) + shape,
                            lambda b, j, pt: (layer, pt[b * n_pages + j * pp + n], 0, 0))

    grid_spec = pltpu.PrefetchScalarGridSpec(
        num_scalar_prefetch=1, grid=(dec_batch, n_pages // pp),
        in_specs=[rows, rows, rows, small((2, qk)), small((2, qk)), small((1, hw))]
                 + [page_spec(n, (aw, page)) for n in range(pp)]
                 + [page_spec(n, (page * heads, hw)) for n in range(pp)],
        out_specs=rows,
        scratch_shapes=[pltpu.VMEM((hm * l, aw), BF16), pltpu.VMEM((hm * l, 1), F32),
                        pltpu.VMEM((hm * l, 1), F32), pltpu.VMEM((hm * l, hw), F32)])
    return pl.pallas_call(
        kern, grid_spec=grid_spec, out_shape=jax.ShapeDtypeStruct((ms, aw), F32),
        compiler_params=_params(("parallel", "arbitrary")), name="attn_sample",
    )(page_table.reshape(-1), qa, ka, va, lq, lk, g_eff.reshape(1, hw),
      *([cache_kt] * pp), *([cache_vr] * pp))


def _gla_kernel(q_ref, k_ref, v_ref, la_ref, sr_ref, bn_ref, s0_ref, o_ref, sf_ref, st_ref,
                *, rows, chunk, sub, heads, dk, dv, nseq):
    j = pl.program_id(1)
    kw = heads * dk
    vw = heads * dv
    cps = rows // chunk // nseq

    def load_state(n):
        for h in range(heads):
            st_ref[:, h * dk:(h + 1) * dk] = s0_ref[n, h].T

    if nseq == 1:
        pl.when(j == 0)(lambda: load_state(0))

    tri = (lax.broadcasted_iota(jnp.int32, (chunk, chunk), 0)
           >= lax.broadcasted_iota(jnp.int32, (chunk, chunk), 1)).astype(BF16)
    head_of = lambda n: lax.broadcasted_iota(jnp.int32, (n, kw), 1) // dk

    def stack_heads(x):
        lh = head_of(x.shape[0])
        return jnp.concatenate([jnp.where(lh == h, x, 0.0) for h in range(heads)], axis=0).astype(BF16)

    starts = list(range(0, rows, chunk))
    subs = list(range(0, chunk, sub))
    qcs = [q_ref[c0:c0 + chunk, :] for c0 in starts]
    kcs = [k_ref[c0:c0 + chunk, :] for c0 in starts]
    vcs = [v_ref[c0:c0 + chunk, :].astype(BF16) for c0 in starts]

    def decay_sum(la):
        hi = la.astype(BF16)
        r1 = la - hi.astype(F32)
        mid = r1.astype(BF16)
        lo = (r1 - mid.astype(F32)).astype(BF16)
        parts = _dot(tri, jnp.concatenate([hi, mid, lo], axis=1))
        return parts[:, :kw] + parts[:, kw:2 * kw] + parts[:, 2 * kw:]

    bs = [decay_sum(la_ref[c0:c0 + chunk, :]) for c0 in starts]
    qsts, kds, decs, qss, kis = [], [], [], [], []
    for qc, kc, b in zip(qcs, kcs, bs):
        b_last = b[chunk - 1:chunk, :]
        qsts.append(stack_heads(qc * jnp.exp(b)))
        kds.append(stack_heads(kc * jnp.exp(b_last - b)))
        decs.append(jnp.exp(b_last))
        for lo in subs:
            hi = lo + sub
            ref = b[lo - 1:lo, :] if lo > 0 else jnp.zeros((1, kw), F32)
            qss.append(stack_heads(qc[lo:hi, :] * jnp.exp(b[lo:hi, :] - ref)))
            kis.append((kc[:hi, :] * jnp.exp(ref - b[:hi, :])).astype(BF16))
    vsts = [jnp.concatenate([vc[:, h * dv:(h + 1) * dv] for h in range(heads)], axis=0) for vc in vcs]
    upds = [_dot_tn(vst, kd) for vst, kd in zip(vsts, kds)]
    atts = [_dot_nt(qs, ki) for qs, ki in zip(qss, kis)]
    masked = []
    for n, att in enumerate(atts):
        lo = subs[n % len(subs)]
        tt = lax.broadcasted_iota(jnp.int32, att.shape, 0) % sub + lo
        ss = lax.broadcasted_iota(jnp.int32, att.shape, 1)
        masked.append(jnp.where(ss <= tt, att, 0.0).astype(BF16))
    ovs = [_dot(att, vcs[n // len(subs)][:subs[n % len(subs)] + sub, :]) for n, att in enumerate(masked)]
    pre = []
    for c in range(len(starts)):
        parts = [jnp.concatenate([ov[h * sub:(h + 1) * sub, h * dv:(h + 1) * dv] for h in range(heads)], axis=1)
                 for ov in ovs[c * len(subs):(c + 1) * len(subs)]]
        intra = jnp.concatenate(parts, axis=0) if len(parts) > 1 else parts[0]
        pre.append((qsts[c], upds[c], decs[c], intra))

    st = st_ref[...] if nseq == 1 else None
    for n, c0 in enumerate(range(0, rows, chunk)):
        if nseq > 1 and n % cps == 0:
            load_state(n // cps)
            st = st_ref[...]
        qst, upd, dec, intra = pre[n]
        r = _dot_nt(qst, st.astype(BF16))
        o = jnp.concatenate([r[h * chunk:(h + 1) * chunk, :] for h in range(heads)], axis=1) + intra
        st = st * dec + upd
        sr = sr_ref[c0:c0 + chunk, :].astype(F32)
        outs = [_rms(o[:, h * dv:(h + 1) * dv], bn_ref[...]) * sr[:, h * dv:(h + 1) * dv]
                for h in range(heads)]
        o_ref[c0:c0 + chunk, :] = jnp.concatenate(outs, axis=1).astype(o_ref.dtype)
        if nseq > 1 and n % cps == cps - 1:
            for h in range(heads):
                sf_ref[n // cps, h] = st[:, h * dk:(h + 1) * dk].T

    if nseq == 1:
        st_ref[...] = st

        @pl.when(j == pl.num_programs(1) - 1)
        def _():
            for h in range(heads):
                sf_ref[0, h] = st[:, h * dk:(h + 1) * dk].T


def _gla(qb, kb, vb, la, sr, bn, s0, batch, seq, out_dtype, chunk_target=64, rows_target=1024):
    m, kw = qb.shape
    vw = vb.shape[1]
    _, heads, dk, dv = s0.shape
    chunk = min(chunk_target, seq)
    assert seq % chunk == 0 and chunk % 8 == 0
    sub = min(16, chunk)
    assert chunk % sub == 0
    rows = chunk * max(1, min(rows_target // chunk, seq // chunk))
    while seq % rows:
        rows -= chunk
    nj = seq // rows
    nseq = max(n for n in range(1, 9) if batch % n == 0 and n * rows <= rows_target) if nj == 1 else 1
    rows *= nseq
    kern = functools.partial(_gla_kernel, rows=rows, chunk=chunk, sub=sub, heads=heads, dk=dk, dv=dv, nseq=nseq)
    row = lambda w: pl.BlockSpec((rows, w), lambda b, j: (b * nj + j, 0))
    state = pl.BlockSpec((nseq, heads, dk, dv), lambda b, j: (b, 0, 0, 0))
    return pl.pallas_call(
        kern, grid=(batch // nseq, nj),
        in_specs=[row(kw), row(kw), row(vw), row(kw), row(vw),
                  pl.BlockSpec((1, dv), lambda b, j: (0, 0)), state],
        out_specs=[row(vw), state],
        out_shape=[jax.ShapeDtypeStruct((m, vw), out_dtype), jax.ShapeDtypeStruct(s0.shape, F32)],
        scratch_shapes=[pltpu.VMEM((dv, kw), F32)],
        compiler_params=_params(("parallel", "arbitrary")), name="gla",
    )(qb, kb, vb, la, sr, bn.reshape(1, dv), s0)


def kernel(x_prompt, x_sample, cache_k, cache_v, state_gla, page_table,
           ffn1_norm, ffn1_wg, ffn1_wu, ffn1_wd, mix_norm, w_in, w_gate_up, b_gate,
           lam_q, lam_k, a_subln, b_norm, w_pa, w_pb, w_o,
           ffn2_norm, ffn2_wg, ffn2_wu, ffn2_wd, final_norm):
    batch, seq, d = x_prompt.shape
    dec_batch, dec_seq, _ = x_sample.shape
    depth, n_phys, page, a_heads, _, a_qk = cache_k.shape
    a_v = cache_v.shape[-1]
    _, _, b_heads, b_dk, b_dv = state_gla.shape
    rank = w_gate_up.shape[1]
    aw, bkw, bvw = a_heads * a_v, b_heads * b_dk, b_heads * b_dv
    assert a_v == 2 * a_qk and aw == a_heads * 2 * a_qk

    o_gd = 3 * aw + 2 * bkw + bvw
    w_a = w_in[:, :, :o_gd].astype(BF16)
    w_b = w_in[:, :, o_gd + rank:].astype(BF16)
    w_gd = jnp.pad(w_in[:, :, o_gd:o_gd + rank], ((0, 0), (0, 0), (0, LANES - rank))).astype(BF16)
    wgu_f = jnp.concatenate([w_gate_up, jnp.zeros((depth, LANES - rank, bkw), F32)], axis=1)
    wgu_hi = wgu_f.astype(BF16)
    wgu_pad = jnp.concatenate([wgu_hi, (wgu_f - wgu_hi.astype(F32)).astype(BF16)], axis=-1)
    bf = lambda w: w.astype(BF16)
    f1g, f1u, f1d = bf(ffn1_wg), bf(ffn1_wu), bf(ffn1_wd)
    f2g, f2u, f2d = bf(ffn2_wg), bf(ffn2_wu), bf(ffn2_wd)
    wpa, wpb, wo = bf(w_pa), bf(w_pb), bf(w_o)
    ck = jnp.transpose(cache_k, (0, 1, 3, 4, 5, 2)).reshape(depth, n_phys, aw, page)
    cv = cache_v.reshape(depth, n_phys, page * a_heads, a_v)

    xp = x_prompt.reshape(batch * seq, d)
    xs = x_sample.reshape(dec_batch * dec_seq, d)
    s0_prompt = jnp.zeros((batch, b_heads, b_dk, b_dv), F32)
    dims = (aw, bkw, bvw, a_qk, b_dk, a_heads)
    kp, vp, sp, ks, vs, ss = [], [], [], [], [], []
    for l in range(depth):
        lam_init = 0.8 - 0.6 * math.exp(-0.3 * l)
        g_eff = a_subln[l] * (1.0 - lam_init)
        last = l == depth - 1

        xp = _ffn(xp, ffn1_norm[l], f1g[l], f1u[l], f1d[l])
        xs = _ffn(xs, ffn1_norm[l], f1g[l], f1u[l], f1d[l])

        qa, ka, kab, va, vab, qb, kb, vb, la, sr, sma, smb = _win(
            xp, mix_norm[l], (w_a[l], w_b[l], w_gd[l]), wgu_pad[l], b_gate[l], dims, narrow=True, seq=seq)
        oa = _attn_prompt(qa, kab, vab, lam_q[l], lam_k[l], g_eff, batch, seq, a_heads, a_qk, lam_init)
        ob, sfin = _gla(qb, kb, vb, la, sr, b_norm[l], s0_prompt, batch, seq, BF16)
        xp = _ffn(xp, ffn2_norm[l], f2g[l], f2u[l], f2d[l],
                  merge=(oa, ob, sma, smb, wpa[l], wpb[l], wo[l]), final_g=final_norm if last else None)
        kp.append(ka); vp.append(va); sp.append(sfin)

        qa, ka, kab, va, vab, qb, kb, vb, la, sr, sma, smb = _win(
            xs, mix_norm[l], (w_a[l], w_b[l], w_gd[l]), wgu_pad[l], b_gate[l], dims, narrow=False)
        oa = _attn_sample(qa, ka, va, lam_q[l], lam_k[l], g_eff, ck, cv, page_table, l,
                          dec_batch, dec_seq, a_heads, a_qk, lam_init)
        ob, sfin = _gla(qb, kb, vb, la, sr, b_norm[l], state_gla[l], dec_batch, dec_seq, F32)
        xs = _ffn(xs, ffn2_norm[l], f2g[l], f2u[l], f2d[l],
                  merge=(oa, ob, sma, smb, wpa[l], wpb[l], wo[l]), final_g=final_norm if last else None)
        ks.append(ka); vs.append(va); ss.append(sfin)

    y_prompt = xp.reshape(batch, seq, d)
    y_sample = xs.reshape(dec_batch, dec_seq, d)
    k_prompt = jnp.transpose(jnp.stack(kp).reshape(depth, batch, a_heads, 2, a_qk, seq), (0, 1, 5, 2, 3, 4))
    v_prompt = jnp.stack(vp).reshape(depth, batch, seq, a_heads, a_v)
    gla_prompt = jnp.stack(sp)
    k_sample = jnp.stack(ks).reshape(depth, dec_batch, dec_seq, a_heads, 2, a_qk)
    v_sample = jnp.stack(vs).reshape(depth, dec_batch, dec_seq, a_heads, a_v)
    gla_sample = jnp.stack(ss)
    return (y_prompt, y_sample, k_prompt, v_prompt, gla_prompt, k_sample, v_sample, gla_sample)
```
